```python
import jax
import jax.numpy as jnp
from jax import lax
import numpy as np


D_MODEL = 1024
BATCH = 8
SEQ = 4096
DEPTH = 4

CTX_LEN = 256
GRID_W = 64
N_MOD = 9
NORM_EPS = 1e-6
D_FF = 2816
ATTN_HEADS = 8
ATTN_KV_HEADS = 2
ATTN_GROUP = ATTN_HEADS // ATTN_KV_HEADS
ATTN_HEAD_DIM = 128
WINDOW = 128
ATTN_BLOCK = 128
ROPE_THETA = 10000.0
GLA_HEADS = 4
GLA_KEY_DIM = 128
GLA_VALUE_DIM = 256
GLA_GATE_RANK = 16
GLA_GATE_TEMP = 16.0
GLA_CHUNK = 64
ATTN_Q_W = ATTN_HEADS * ATTN_HEAD_DIM
ATTN_KV_W = ATTN_KV_HEADS * ATTN_HEAD_DIM
GLA_K_W = GLA_HEADS * GLA_KEY_DIM
GLA_V_W = GLA_HEADS * GLA_VALUE_DIM
CTX_SIDE_COLS = (ATTN_KV_W, ATTN_KV_W, GLA_K_W, GLA_V_W, GLA_GATE_RANK, GLA_GATE_RANK)
QUERY_SIDE_COLS = (ATTN_Q_W, GLA_K_W, GLA_V_W, D_MODEL, D_MODEL)
N_CTX_SIDE = sum(CTX_SIDE_COLS)
IN_COLS = N_CTX_SIDE + sum(QUERY_SIDE_COLS)

kernel_name = 'hybrid_flow_backbone_block'


def split_cols(z, sizes):
    idx = [int(i) for i in np.cumsum(sizes)[:-1]]
    return jnp.split(z, idx, axis=-1)


def flip_time(t):
    return jnp.flip(t, axis=1)


def rmsnorm(x, w):
    x32 = x.astype(jnp.float32)
    y = x32 * lax.rsqrt(jnp.mean(x32 * x32, axis=-1, keepdims=True) + NORM_EPS)
    return (y * w.astype(jnp.float32)).astype(x.dtype)


def adaln(cond, w, b):
    m = (cond @ w + b)[..., None, :]
    return jnp.split(m, N_MOD, axis=-1)


def modulate(h, shift, scale):
    return h * (1 + scale) + shift


def swiglu(h, w13, w2):
    up, gate = jnp.split(h @ w13, 2, axis=-1)
    return (jax.nn.silu(gate) * up) @ w2


def axial_rope_tables(n_tokens):
    rows = n_tokens // GRID_W
    row = jnp.repeat(jnp.arange(rows), GRID_W).astype(jnp.float32)
    col = jnp.tile(jnp.arange(GRID_W), rows).astype(jnp.float32)
    half = ATTN_HEAD_DIM // 2
    inv_freq = ROPE_THETA ** (-jnp.arange(0, half, 2, dtype=jnp.float32) / half)
    ang_r = row[:, None] * inv_freq[None, :]
    ang_c = col[:, None] * inv_freq[None, :]
    return (jnp.cos(ang_r), jnp.sin(ang_r), jnp.cos(ang_c), jnp.sin(ang_c))


def rotate_pairs(x, cos, sin):
    x1, x2 = jnp.split(x.astype(jnp.float32), 2, axis=-1)
    cos = cos[:, None, :]
    sin = sin[:, None, :]
    return jnp.concatenate([x1 * cos - x2 * sin, x1 * sin + x2 * cos], axis=-1)


def axial_rope(x, tabs):
    cos_r, sin_r, cos_c, sin_c = tabs
    x_row, x_col = jnp.split(x, 2, axis=-1)
    return jnp.concatenate([rotate_pairs(x_row, cos_r, sin_r), rotate_pairs(x_col, cos_c, sin_c)], axis=-1).astype(x.dtype)


def window_attention(q, k, v, kc, vc, sink):
    B, S = q.shape[:2]
    L = kc.shape[1]
    nb = S // ATTN_BLOCK
    scale = ATTN_HEAD_DIM ** -0.5
    qb = q.reshape(B, nb, ATTN_BLOCK, ATTN_KV_HEADS, ATTN_GROUP, ATTN_HEAD_DIM)

    def band(t):
        tp = jnp.pad(t, ((0, 0), (ATTN_BLOCK, ATTN_BLOCK), (0, 0), (0, 0)))
        tp = tp.reshape(B, nb + 2, ATTN_BLOCK, ATTN_KV_HEADS, ATTN_HEAD_DIM)
        return jnp.concatenate([tp[:, :-2], tp[:, 1:-1], tp[:, 2:]], axis=2)

    kw, vw = band(k), band(v)
    qi = jnp.arange(ATTN_BLOCK)[:, None]
    kj = jnp.arange(3 * ATTN_BLOCK)[None, :] - ATTN_BLOCK
    key_pos = jnp.arange(nb)[:, None, None] * ATTN_BLOCK + kj[None]
    mask = (jnp.abs(kj - qi) <= WINDOW)[None] & (key_pos >= 0) & (key_pos < S)
    s_win = jnp.einsum('bnqhgd,bnkhd->bnhgqk', qb, kw).astype(jnp.float32) * scale
    s_win = jnp.where(mask[None, :, None, None], s_win, -jnp.inf)
    s_ctx = jnp.einsum('bnqhgd,blhd->bnhgql', qb, kc).astype(jnp.float32) * scale
    s_sink = jnp.broadcast_to(sink.reshape(ATTN_KV_HEADS, ATTN_GROUP, 1, 1).astype(jnp.float32), s_ctx.shape[:-1] + (1,))
    p = jax.nn.softmax(jnp.concatenate([s_win, s_ctx, s_sink], axis=-1), axis=-1).astype(v.dtype)
    p_win = p[..., :3 * ATTN_BLOCK]
    p_ctx = p[..., 3 * ATTN_BLOCK:3 * ATTN_BLOCK + L]
    o = jnp.einsum('bnhgqk,bnkhd->bnqhgd', p_win, vw) + jnp.einsum('bnhgql,blhd->bnqhgd', p_ctx, vc)
    return o.reshape(B, S, ATTN_Q_W)


def context_attention(qc, kc, vc, sink):
    B, L = qc.shape[:2]
    scale = ATTN_HEAD_DIM ** -0.5
    qg = qc.reshape(B, L, ATTN_KV_HEADS, ATTN_GROUP, ATTN_HEAD_DIM)
    s = jnp.einsum('blhgd,bmhd->bhglm', qg, kc).astype(jnp.float32) * scale
    s_sink = jnp.broadcast_to(sink.reshape(ATTN_KV_HEADS, ATTN_GROUP, 1, 1).astype(jnp.float32), s.shape[:-1] + (1,))
    p = jax.nn.softmax(jnp.concatenate([s, s_sink], axis=-1), axis=-1).astype(vc.dtype)
    o = jnp.einsum('bhglm,bmhd->blhgd', p[..., :L], vc)
    return o.reshape(B, L, ATTN_Q_W)


def gla_log_decay(lowrank, w, b):
    B, T = lowrank.shape[:2]
    logit = (lowrank @ w + b).astype(jnp.float32)
    return (jax.nn.log_sigmoid(logit) / GLA_GATE_TEMP).reshape(B, T, GLA_HEADS, GLA_KEY_DIM)


def gla_chunk_scan(q, k, v, g, s0):
    B, T, H, dk = q.shape
    dv = v.shape[-1]
    nc = T // GLA_CHUNK
    out_dtype = v.dtype

    def chunks(t):
        return jnp.moveaxis(t.astype(jnp.float32).reshape(B, nc, GLA_CHUNK, H, t.shape[-1]), 1, 0)

    causal = jnp.tril(jnp.ones((GLA_CHUNK, GLA_CHUNK), dtype=bool))

    def step(s, inp):
        qc, kc, vc, gc = inp
        b = jnp.cumsum(gc, axis=1)
        o_inter = jnp.einsum('bchd,bhde->bche', qc * jnp.exp(b), s)
        diff = b[:, :, None] - b[:, None, :]
        decay = jnp.exp(jnp.where(causal[None, :, :, None, None], diff, -jnp.inf))
        a = jnp.einsum('bijhd,bjhd->bhij', qc[:, :, None] * decay, kc)
        o_intra = jnp.einsum('bhij,bjhe->bihe', a, vc)
        b_last = b[:, -1]
        s_new = jnp.exp(b_last)[..., None] * s + jnp.einsum('bchd,bche->bhde', kc * jnp.exp(b_last[:, None] - b), vc)
        return s_new, o_inter + o_intra

    s_final, o = lax.scan(step, s0.astype(jnp.float32), (chunks(q), chunks(k), chunks(v), chunks(g)))
    o = jnp.moveaxis(o, 0, 1).reshape(B, T, H, dv).astype(out_dtype)
    return o, s_final


def gla_final_state(k, v, g):
    k = k.astype(jnp.float32)
    v = v.astype(jnp.float32)
    b = jnp.cumsum(g.astype(jnp.float32), axis=1)
    w = jnp.exp(b[:, -1:] - b)
    return jnp.einsum('bthd,bthe->bhde', k * w, v)


def gla_output(o, r, w):
    B, T = r.shape[:2]
    return rmsnorm(o, w).reshape(B, T, GLA_V_W) * jax.nn.silu(r)


def merge_branches(y_attn, y_gla, gate_a, gate_g, w_ba, w_bg, w_out):
    return (jax.nn.sigmoid(gate_a) * (y_attn @ w_ba) + jax.nn.sigmoid(gate_g) * (y_gla @ w_bg)) @ w_out


def mixer(h, hc, w_in, q_norm_w, k_norm_w, sink, gw_f, gb_f, gw_b, gb_b, gla_norm_w, w_ba, w_bg, w_out, tabs, need_ctx_out):
    B, S, _ = h.shape
    L = hc.shape[1]
    ak, av, gk, gv, glr_f, glr_b, aq, gq, gr, gate_a, gate_g = split_cols(h @ w_in, CTX_SIDE_COLS + QUERY_SIDE_COLS)
    akc, avc, gkc, gvc, glrc_f, glrc_b = split_cols(hc @ w_in[:, :N_CTX_SIDE], CTX_SIDE_COLS)

    q = axial_rope(rmsnorm(aq.reshape(B, S, ATTN_HEADS, ATTN_HEAD_DIM), q_norm_w), tabs)
    k = axial_rope(rmsnorm(ak.reshape(B, S, ATTN_KV_HEADS, ATTN_HEAD_DIM), k_norm_w), tabs)
    v = av.reshape(B, S, ATTN_KV_HEADS, ATTN_HEAD_DIM)
    kc = rmsnorm(akc.reshape(B, L, ATTN_KV_HEADS, ATTN_HEAD_DIM), k_norm_w)
    vc = avc.reshape(B, L, ATTN_KV_HEADS, ATTN_HEAD_DIM)
    y_attn = window_attention(q, k, v, kc, vc, sink)

    gla_scale = GLA_KEY_DIM ** -0.5
    qg = gq.reshape(B, S, GLA_HEADS, GLA_KEY_DIM) * gla_scale
    kg = gk.reshape(B, S, GLA_HEADS, GLA_KEY_DIM)
    vg = gv.reshape(B, S, GLA_HEADS, GLA_VALUE_DIM)
    g_f = gla_log_decay(glr_f, gw_f, gb_f)
    g_b = gla_log_decay(glr_b, gw_b, gb_b)
    kgc = gkc.reshape(B, L, GLA_HEADS, GLA_KEY_DIM)
    vgc = gvc.reshape(B, L, GLA_HEADS, GLA_VALUE_DIM)
    gc_f = gla_log_decay(glrc_f, gw_f, gb_f)
    gc_b = gla_log_decay(glrc_b, gw_b, gb_b)

    if need_ctx_out:
        aqc, gqc, grc, gate_ac, gate_gc = split_cols(hc @ w_in[:, N_CTX_SIDE:], QUERY_SIDE_COLS)
        qc = rmsnorm(aqc.reshape(B, L, ATTN_HEADS, ATTN_HEAD_DIM), q_norm_w)
        yc_attn = context_attention(qc, kc, vc, sink)
        qgc = gqc.reshape(B, L, GLA_HEADS, GLA_KEY_DIM) * gla_scale
        zero = jnp.zeros((B, GLA_HEADS, GLA_KEY_DIM, GLA_VALUE_DIM), jnp.float32)
        oc_f, s_f = gla_chunk_scan(qgc, kgc, vgc, gc_f, zero)
        oc_b, s_b = gla_chunk_scan(flip_time(qgc), flip_time(kgc), flip_time(vgc), flip_time(gc_b), zero)
        yc_gla = gla_output(oc_f + flip_time(oc_b), grc, gla_norm_w)
        yc = merge_branches(yc_attn, yc_gla, gate_ac, gate_gc, w_ba, w_bg, w_out)
    else:
        s_f = gla_final_state(kgc, vgc, gc_f)
        s_b = gla_final_state(flip_time(kgc), flip_time(vgc), flip_time(gc_b))
        yc = None

    o_f, _ = gla_chunk_scan(qg, kg, vg, g_f, s_f)
    o_b, _ = gla_chunk_scan(flip_time(qg), flip_time(kg), flip_time(vg), flip_time(g_b), s_b)
    y_gla = gla_output(o_f + flip_time(o_b), gr, gla_norm_w)
    y = merge_branches(y_attn, y_gla, gate_a, gate_g, w_ba, w_bg, w_out)
    return y, yc


def setup_inputs(seed: int = 0) -> dict:
    key = jax.random.key(seed)
    ks = jax.random.split(key, 24)
    f32 = jnp.float32

    def nrm(k, shape, scale):
        return jax.random.normal(k, shape, f32) * scale

    return {
        'x': nrm(ks[0], (BATCH, SEQ, D_MODEL), 1.0),
        'c': nrm(ks[1], (BATCH, D_MODEL), 1.0),
        'ctx': nrm(ks[2], (BATCH, CTX_LEN, D_MODEL), 1.0),
        'c_ctx': nrm(ks[3], (D_MODEL,), 1.0),
        'w_mod': nrm(ks[4], (DEPTH, D_MODEL, N_MOD * D_MODEL), 0.5 * D_MODEL ** -0.5),
        'b_mod': nrm(ks[5], (DEPTH, N_MOD * D_MODEL), 0.02),
        'norm_w': 1.0 + nrm(ks[6], (DEPTH, 3, D_MODEL), 0.02),
        'ffn1_w13': nrm(ks[7], (DEPTH, D_MODEL, 2 * D_FF), D_MODEL ** -0.5),
        'ffn1_w2': nrm(ks[8], (DEPTH, D_FF, D_MODEL), D_FF ** -0.5),
        'ffn2_w13': nrm(ks[9], (DEPTH, D_MODEL, 2 * D_FF), D_MODEL ** -0.5),
        'ffn2_w2': nrm(ks[10], (DEPTH, D_FF, D_MODEL), D_FF ** -0.5),
        'w_in': nrm(ks[11], (DEPTH, D_MODEL, IN_COLS), D_MODEL ** -0.5),
        'q_norm_w': 1.0 + nrm(ks[12], (DEPTH, ATTN_HEAD_DIM), 0.02),
        'k_norm_w': 1.0 + nrm(ks[13], (DEPTH, ATTN_HEAD_DIM), 0.02),
        'attn_sink': nrm(ks[14], (DEPTH, ATTN_HEADS), 0.5),
        'gla_gate_w_fwd': nrm(ks[15], (DEPTH, GLA_GATE_RANK, GLA_K_W), GLA_GATE_RANK ** -0.5),
        'gla_gate_b_fwd': nrm(ks[16], (DEPTH, GLA_K_W), 0.1),
        'gla_gate_w_bwd': nrm(ks[17], (DEPTH, GLA_GATE_RANK, GLA_K_W), GLA_GATE_RANK ** -0.5),
        'gla_gate_b_bwd': nrm(ks[18], (DEPTH, GLA_K_W), 0.1),
        'gla_norm_w': 1.0 + nrm(ks[19], (DEPTH, GLA_VALUE_DIM), 0.02),
        'w_branch_attn': nrm(ks[20], (DEPTH, ATTN_Q_W, D_MODEL), ATTN_Q_W ** -0.5),
        'w_branch_gla': nrm(ks[21], (DEPTH, GLA_V_W, D_MODEL), GLA_V_W ** -0.5),
        'w_out': nrm(ks[22], (DEPTH, D_MODEL, D_MODEL), D_MODEL ** -0.5),
    }


def reference(x, c, ctx, c_ctx, w_mod, b_mod, norm_w, ffn1_w13, ffn1_w2, ffn2_w13, ffn2_w2, w_in, q_norm_w, k_norm_w, attn_sink, gla_gate_w_fwd, gla_gate_b_fwd, gla_gate_w_bwd, gla_gate_b_bwd, gla_norm_w, w_branch_attn, w_branch_gla, w_out):
    tabs = axial_rope_tables(x.shape[1])
    cond = jax.nn.silu(c)
    cond_ctx = jax.nn.silu(c_ctx)
    xc = ctx
    for l in range(DEPTH):
        last = l == DEPTH - 1
        m = adaln(cond, w_mod[l], b_mod[l])
        mc = adaln(cond_ctx, w_mod[l], b_mod[l])
        x = x + 0.5 * m[2] * swiglu(modulate(rmsnorm(x, norm_w[l, 0]), m[0], m[1]), ffn1_w13[l], ffn1_w2[l])
        xc = xc + 0.5 * mc[2] * swiglu(modulate(rmsnorm(xc, norm_w[l, 0]), mc[0], mc[1]), ffn1_w13[l], ffn1_w2[l])
        h = modulate(rmsnorm(x, norm_w[l, 1]), m[3], m[4])
        hc = modulate(rmsnorm(xc, norm_w[l, 1]), mc[3], mc[4])
        y, yc = mixer(h, hc, w_in[l], q_norm_w[l], k_norm_w[l], attn_sink[l],
                      gla_gate_w_fwd[l], gla_gate_b_fwd[l], gla_gate_w_bwd[l], gla_gate_b_bwd[l],
                      gla_norm_w[l], w_branch_attn[l], w_branch_gla[l], w_out[l], tabs, not last)
        x = x + m[5] * y
        x = x + 0.5 * m[8] * swiglu(modulate(rmsnorm(x, norm_w[l, 2]), m[6], m[7]), ffn2_w13[l], ffn2_w2[l])
        if not last:
            xc = xc + mc[5] * yc
            xc = xc + 0.5 * mc[8] * swiglu(modulate(rmsnorm(xc, norm_w[l, 2]), mc[6], mc[7]), ffn2_w13[l], ffn2_w2[l])
    return x
```

```python
import functools

import numpy as np
import jax
import jax.numpy as jnp
from jax import lax
from jax.experimental import pallas as pl
from jax.experimental.pallas import tpu as pltpu

F32 = jnp.float32
BF16 = jnp.bfloat16

D_MODEL = 1024
N_MOD = 9
NORM_EPS = 1e-6
D_FF = 2816
GRID_W = 64
ATTN_HEADS = 8
ATTN_KV_HEADS = 2
ATTN_GROUP = ATTN_HEADS // ATTN_KV_HEADS
HEAD_DIM = 128
ATTN_BLOCK = 128
ROPE_THETA = 10000.0
GLA_HEADS = 4
GLA_DK = 128
GLA_DV = 256
GLA_RANK = 16
GLA_TEMP = 16.0
GLA_CHUNK = 64
ATTN_Q_W = ATTN_HEADS * HEAD_DIM
ATTN_KV_W = ATTN_KV_HEADS * HEAD_DIM
GLA_K_W = GLA_HEADS * GLA_DK
GLA_V_W = GLA_HEADS * GLA_DV
GLA_SAFE_LOG_DECAY = -60.0

V7X_VMEM_BYTES = 64 * 1024 * 1024
VMEM_LIMIT = V7X_VMEM_BYTES - 8 * 1024 * 1024
MOD_ROWS = 16
MOD_COL_TILE = 1024
SUBLANES = 8


def _dot(a, b):
    return jnp.dot(a, b, preferred_element_type=F32)


def _dot_nt(a, b):
    return lax.dot_general(a, b, (((1,), (1,)), ((), ())), preferred_element_type=F32)


def _dot_tn(a, b):
    return lax.dot_general(a, b, (((0,), (0,)), ((), ())), preferred_element_type=F32)


def _sigmoid(x):
    return 1.0 / (1.0 + jnp.exp(-x))


def _resident(block_shape, index_map):
    return pl.BlockSpec(block_shape, index_map, pipeline_mode=pl.Buffered(1))


def _params(*sem):
    return pltpu.CompilerParams(dimension_semantics=sem, vmem_limit_bytes=VMEM_LIMIT)


class _Layout:
    def __init__(self, batch, seq, ctx_len):
        self.b, self.s, self.l = batch, seq, ctx_len
        self.n_ctx = batch * ctx_len
        self.n = self.n_ctx + batch * seq
        tm = 512
        while self.n_ctx % tm or seq % tm:
            tm //= 2
        assert tm >= GLA_CHUNK and tm % GLA_CHUNK == 0
        self.tm = tm
        self.ctx_tiles = self.n_ctx // tm
        self.tiles_per_batch = seq // tm
        self.tiles = self.n // tm
        tb = 256
        while ctx_len % tb or seq % tb:
            tb //= 2
        assert tb >= GLA_CHUNK
        self.tb = tb
        assert ctx_len % ATTN_BLOCK == 0 and seq % ATTN_BLOCK == 0 and seq % GRID_W == 0

    def mod_row(self, i):
        return jnp.where(i < self.ctx_tiles, self.b, (i - self.ctx_tiles) // self.tiles_per_batch)

    def rope_block(self, i):
        return jnp.where(i < self.ctx_tiles, 0, 1 + (i - self.ctx_tiles) % self.tiles_per_batch)


def _mod_kernel(c_ref, w_ref, b_ref, o_ref):
    c = c_ref[...]
    cond = (c * _sigmoid(c)).astype(BF16)
    o_ref[...] = _dot(cond, w_ref[...].astype(BF16)) + b_ref[...]


def _modulation(cond_rows, w_mod, b_mod):
    depth = w_mod.shape[0]
    n_cols = N_MOD * D_MODEL
    return pl.pallas_call(
        _mod_kernel,
        grid=(depth, n_cols // MOD_COL_TILE),
        in_specs=[
            pl.BlockSpec((MOD_ROWS, D_MODEL), lambda l, j: (0, 0)),
            pl.BlockSpec((None, D_MODEL, MOD_COL_TILE), lambda l, j: (l, 0, j)),
            pl.BlockSpec((None, 1, MOD_COL_TILE), lambda l, j: (l, 0, j)),
        ],
        out_specs=pl.BlockSpec((None, MOD_ROWS, MOD_COL_TILE), lambda l, j: (l, 0, j)),
        out_shape=jax.ShapeDtypeStruct((depth, MOD_ROWS, n_cols), F32),
        compiler_params=_params("arbitrary", "arbitrary"),
    )(cond_rows, w_mod, b_mod.reshape(depth, 1, n_cols))


def _mod_slice(mod_ref, k):
    return mod_ref[:, k * D_MODEL:(k + 1) * D_MODEL]


def _norm_modulate(x, nw, shift, scale):
    y = x * lax.rsqrt(jnp.mean(x * x, axis=-1, keepdims=True) + NORM_EPS) * nw
    return y * (1.0 + scale) + shift


FFN_CHUNKS = 2


def _ffn_kernel(x_ref, mod_ref, nw_ref, w13_ref, w2_ref, o_ref, *, sub):
    x = x_ref[...]
    h = _norm_modulate(x, nw_ref[sub:sub + 1, :], _mod_slice(mod_ref, 3 * sub), _mod_slice(mod_ref, 3 * sub + 1))
    hb = h.astype(BF16)
    fc = D_FF // FFN_CHUNKS
    acc = None
    for c in range(FFN_CHUNKS):
        up = _dot(hb, w13_ref[:, c * fc:(c + 1) * fc])
        gate = _dot(hb, w13_ref[:, D_FF + c * fc:D_FF + (c + 1) * fc])
        a = (gate * _sigmoid(gate) * up).astype(BF16)
        part = _dot(a, w2_ref[c * fc:(c + 1) * fc, :])
        acc = part if acc is None else acc + part
    o_ref[...] = x + (0.5 * _mod_slice(mod_ref, 3 * sub + 2)) * acc


def _ffn(lay, layer, sub, x_all, mod, norm_w, w13, w2, *, latents_only=False):
    tm = lay.tm
    off = lay.ctx_tiles if latents_only else 0
    n_tiles = lay.tiles - off
    return pl.pallas_call(
        functools.partial(_ffn_kernel, sub=sub),
        grid=(n_tiles,),
        in_specs=[
            pl.BlockSpec((tm, D_MODEL), lambda i: (i + off, 0)),
            pl.BlockSpec((None, None, 1, N_MOD * D_MODEL), lambda i: (layer, lay.mod_row(i + off), 0, 0)),
            pl.BlockSpec((None, 3, D_MODEL), lambda i: (layer, 0, 0)),
            _resident((None, D_MODEL, 2 * D_FF), lambda i: (layer, 0, 0)),
            _resident((None, D_FF, D_MODEL), lambda i: (layer, 0, 0)),
        ],
        out_specs=pl.BlockSpec((tm, D_MODEL), lambda i: (i, 0)),
        out_shape=jax.ShapeDtypeStruct((n_tiles * tm, D_MODEL), F32),
        compiler_params=_params("arbitrary"),
    )(x_all, mod, norm_w, w13, w2)


C_AK, C_AV, C_GK, C_GV, C_AQ, C_GQ, C_GR, C_GA, C_GG, C_END = np.cumsum(
    [0, ATTN_KV_W, ATTN_KV_W, GLA_K_W, GLA_V_W, ATTN_Q_W, GLA_K_W, GLA_V_W, D_MODEL, D_MODEL]).tolist()
LR_PAD = 128
CUMSUM_ROWS = 256


def _log_sigmoid(x):
    return jnp.minimum(x, 0.0) - jnp.log(1.0 + jnp.exp(-jnp.abs(x)))


def _proj_kernel(x_ref, mod_ref, nw_ref, wm_ref, wlr_ref, gw_ref, gb_ref, qnw_ref, knw_ref, rc_ref, rs_ref,
                 q_ref, kv_ref, gq_ref, gk_ref, gv_ref, bdec_ref, sr_ref, sga_ref, sgg_ref):
    x = x_ref[...]
    tm = x.shape[0]
    h = _norm_modulate(x, nw_ref[1:2, :], _mod_slice(mod_ref, 3), _mod_slice(mod_ref, 4))
    hb = h.astype(BF16)
    rope_c = rc_ref[...]
    rope_s = rs_ref[...]
    lane = lax.broadcasted_iota(jnp.int32, (tm, HEAD_DIM), 1)
    first_half = (lane % (HEAD_DIM // 2)) < (HEAD_DIM // 4)

    def norm_rope(t, w):
        y = t * lax.rsqrt(jnp.mean(t * t, axis=-1, keepdims=True) + NORM_EPS) * w
        partner = jnp.where(first_half, pltpu.roll(y, HEAD_DIM - HEAD_DIM // 4, 1), pltpu.roll(y, HEAD_DIM // 4, 1))
        return y * rope_c + partner * rope_s

    kvp = _dot(hb, wm_ref[:, C_AK:C_GK])
    knw = knw_ref[...]
    for i in range(ATTN_KV_HEADS):
        sl = slice(i * HEAD_DIM, (i + 1) * HEAD_DIM)
        kv_ref[:, sl] = norm_rope(kvp[:, sl], knw).astype(BF16)
    kv_ref[:, ATTN_KV_W:] = kvp[:, ATTN_KV_W:].astype(BF16)
    gk_ref[...] = _dot(hb, wm_ref[:, C_GK:C_GV]).astype(BF16)
    gv_ref[...] = _dot(hb, wm_ref[:, C_GV:C_AQ]).astype(BF16)
    qp = _dot(hb, wm_ref[:, C_AQ:C_GQ])
    qnw = qnw_ref[...]
    for i in range(ATTN_HEADS):
        sl = slice(i * HEAD_DIM, (i + 1) * HEAD_DIM)
        q_ref[:, sl] = norm_rope(qp[:, sl], qnw).astype(BF16)
    gq_ref[...] = (_dot(hb, wm_ref[:, C_GQ:C_GR]) * (GLA_DK ** -0.5)).astype(BF16)
    r = _dot(hb, wm_ref[:, C_GR:C_GA])
    sr_ref[...] = (r * _sigmoid(r)).astype(BF16)
    sga_ref[...] = _sigmoid(_dot(hb, wm_ref[:, C_GA:C_GG])).astype(BF16)
    sgg_ref[...] = _sigmoid(_dot(hb, wm_ref[:, C_GG:C_END])).astype(BF16)

    lr = _dot(hb, wlr_ref[...]).astype(BF16)
    g = _log_sigmoid(_dot(lr, gw_ref[...]) + gb_ref[...]) * (1.0 / GLA_TEMP)
    cr = min(CUMSUM_ROWS, tm)
    ri = lax.broadcasted_iota(jnp.int32, (cr, cr), 0)
    ci = lax.broadcasted_iota(jnp.int32, (cr, cr), 1)
    same = (ri // GLA_CHUNK) == (ci // GLA_CHUNK)
    t_fwd = jnp.where(same & (ci <= ri), 1.0, 0.0).astype(BF16)
    t_bwd = jnp.where(same & (ci >= ri), 1.0, 0.0).astype(BF16)
    for blk in range(tm // cr):
        rows = slice(blk * cr, (blk + 1) * cr)
        for tri, cols in ((t_fwd, slice(0, GLA_K_W)), (t_bwd, slice(GLA_K_W, 2 * GLA_K_W))):
            gp = g[rows, cols]
            hi = gp.astype(BF16)
            lo = (gp - hi.astype(F32)).astype(BF16)
            bdec_ref[rows, cols] = _dot(tri, hi) + _dot(tri, lo)


def _proj(lay, layer, x_all, mod, norm_w, wm, wlr, gw, gb, qnw, knw, rope_c, rope_s):
    tm, n = lay.tm, lay.n
    row = lambda w: pl.BlockSpec((tm, w), lambda i: (i, 0))
    widths = [ATTN_Q_W, 2 * ATTN_KV_W, GLA_K_W, GLA_K_W, GLA_V_W, 2 * GLA_K_W, GLA_V_W, D_MODEL, D_MODEL]
    dtypes = [BF16, BF16, BF16, BF16, BF16, F32, BF16, BF16, BF16]
    return pl.pallas_call(
        _proj_kernel,
        grid=(lay.tiles,),
        in_specs=[
            row(D_MODEL),
            pl.BlockSpec((None, None, 1, N_MOD * D_MODEL), lambda i: (layer, lay.mod_row(i), 0, 0)),
            pl.BlockSpec((None, 3, D_MODEL), lambda i: (layer, 0, 0)),
            _resident((None, D_MODEL, C_END), lambda i: (layer, 0, 0)),
            _resident((None, D_MODEL, LR_PAD), lambda i: (layer, 0, 0)),
            _resident((None, LR_PAD, 2 * GLA_K_W), lambda i: (layer, 0, 0)),
            pl.BlockSpec((None, 1, 2 * GLA_K_W), lambda i: (layer, 0, 0)),
            pl.BlockSpec((None, 1, HEAD_DIM), lambda i: (layer, 0, 0)),
            pl.BlockSpec((None, 1, HEAD_DIM), lambda i: (layer, 0, 0)),
            pl.BlockSpec((tm, HEAD_DIM), lambda i: (lay.rope_block(i), 0)),
            pl.BlockSpec((tm, HEAD_DIM), lambda i: (lay.rope_block(i), 0)),
        ],
        out_specs=[row(w) for w in widths],
        out_shape=[jax.ShapeDtypeStruct((n, w), dt) for w, dt in zip(widths, dtypes)],
        compiler_params=_params("arbitrary"),
    )(x_all, mod, norm_w, wm, wlr, gw, gb, qnw, knw, rope_c, rope_s)


def _rope_tables(lay):
    pos = np.arange(lay.s)
    half = HEAD_DIM // 2
    inv_freq = ROPE_THETA ** (-np.arange(0, half, 2, dtype=np.float32) / half)
    inv_freq = jnp.asarray(inv_freq, F32)
    ang_r = jnp.asarray(pos // GRID_W, F32)[:, None] * inv_freq[None, :]
    ang_c = jnp.asarray(pos % GRID_W, F32)[:, None] * inv_freq[None, :]
    cos = jnp.concatenate([jnp.cos(ang_r)] * 2 + [jnp.cos(ang_c)] * 2, axis=-1)
    sin = jnp.concatenate([-jnp.sin(ang_r), jnp.sin(ang_r), -jnp.sin(ang_c), jnp.sin(ang_c)], axis=-1)
    cos = jnp.concatenate([jnp.ones((lay.tm, HEAD_DIM), F32), cos], axis=0)
    sin = jnp.concatenate([jnp.zeros((lay.tm, HEAD_DIM), F32), sin], axis=0)
    return cos, sin


NEG_BIG = -1e30


def _attn_kernel(sink_ref, q_ref, kp_ref, kc_ref, kn_ref, kx_ref, o_ref, *, layer, ctx_blocks, lat_blocks):
    j = pl.program_id(1)
    jl = j - ctx_blocks
    qb = ATTN_BLOCK
    rows = ATTN_GROUP * qb
    scale = HEAD_DIM ** -0.5
    row_i = lax.broadcasted_iota(jnp.int32, (rows, qb), 0) % qb
    col_i = lax.broadcasted_iota(jnp.int32, (rows, qb), 1)
    grp = lax.broadcasted_iota(jnp.int32, (rows, 1), 0) // qb

    for kvh in range(ATTN_KV_HEADS):
        heads = [kvh * ATTN_GROUP + g for g in range(ATTN_GROUP)]
        q = jnp.concatenate([q_ref[:, hd * HEAD_DIM:(hd + 1) * HEAD_DIM] for hd in heads], axis=0)
        sink = jnp.zeros((rows, 1), F32)
        for g, hd in enumerate(heads):
            sink = jnp.where(grp == g, sink_ref[layer, hd], sink)
        ksl = slice(kvh * HEAD_DIM, (kvh + 1) * HEAD_DIM)
        vsl = slice(ATTN_KV_W + kvh * HEAD_DIM, ATTN_KV_W + (kvh + 1) * HEAD_DIM)
        s_x = _dot_nt(q, kx_ref[:, ksl]) * scale
        m_x = jnp.maximum(jnp.max(s_x, axis=-1, keepdims=True), sink)

        def write(o):
            for g, hd in enumerate(heads):
                o_ref[:, hd * HEAD_DIM:(hd + 1) * HEAD_DIM] = o[g * qb:(g + 1) * qb].astype(o_ref.dtype)

        @pl.when(j < ctx_blocks)
        def _():
            p = jnp.exp(s_x - m_x)
            den = jnp.sum(p, axis=-1, keepdims=True) + jnp.exp(sink - m_x)
            write(_dot(p.astype(BF16), kx_ref[:, vsl]) / den)

        @pl.when(j >= ctx_blocks)
        def _():
            s_p = jnp.where((col_i >= row_i) & (jl > 0), _dot_nt(q, kp_ref[:, ksl]) * scale, NEG_BIG)
            s_c = _dot_nt(q, kc_ref[:, ksl]) * scale
            s_n = jnp.where((col_i <= row_i) & (jl < lat_blocks - 1), _dot_nt(q, kn_ref[:, ksl]) * scale, NEG_BIG)
            m = jnp.maximum(jnp.maximum(jnp.max(s_p, axis=-1, keepdims=True), jnp.max(s_c, axis=-1, keepdims=True)),
                            jnp.maximum(jnp.max(s_n, axis=-1, keepdims=True), m_x))
            p_p, p_c, p_n, p_x = (jnp.exp(s - m) for s in (s_p, s_c, s_n, s_x))
            den = (jnp.sum(p_p, axis=-1, keepdims=True) + jnp.sum(p_c, axis=-1, keepdims=True)
                   + jnp.sum(p_n, axis=-1, keepdims=True) + jnp.sum(p_x, axis=-1, keepdims=True) + jnp.exp(sink - m))
            acc = (_dot(p_p.astype(BF16), kp_ref[:, vsl]) + _dot(p_c.astype(BF16), kc_ref[:, vsl])
                   + _dot(p_n.astype(BF16), kn_ref[:, vsl]) + _dot(p_x.astype(BF16), kx_ref[:, vsl]))
            write(acc / den)


def _attn(lay, layer, sink, q, kv):
    qb = ATTN_BLOCK
    cb, lb = lay.l // qb, lay.s // qb
    lat0 = lay.n_ctx // qb

    def q_blk(b, j):
        return jnp.where(j < cb, b * cb + j, lat0 + b * lb + (j - cb))

    def win_blk(delta):
        def f(b, j):
            return (lat0 + b * lb + jnp.clip(j - cb + delta, 0, lb - 1), 0)
        return f

    return pl.pallas_call(
        functools.partial(_attn_kernel, layer=layer, ctx_blocks=cb, lat_blocks=lb),
        grid=(lay.b, cb + lb),
        in_specs=[
            pl.BlockSpec(memory_space=pltpu.SMEM),
            pl.BlockSpec((qb, ATTN_Q_W), lambda b, j: (q_blk(b, j), 0)),
            pl.BlockSpec((qb, 2 * ATTN_KV_W), win_blk(-1)),
            pl.BlockSpec((qb, 2 * ATTN_KV_W), win_blk(0)),
            pl.BlockSpec((qb, 2 * ATTN_KV_W), win_blk(1)),
            pl.BlockSpec((lay.l, 2 * ATTN_KV_W), lambda b, j: (b, 0)),
        ],
        out_specs=pl.BlockSpec((qb, ATTN_Q_W), lambda b, j: (q_blk(b, j), 0)),
        out_shape=jax.ShapeDtypeStruct((lay.n, ATTN_Q_W), BF16),
        compiler_params=_params("arbitrary", "arbitrary"),
    )(sink, q, kv, kv, kv, kv)


def _gla_chunk(q_ref, k_ref, v_ref, b_ref, o_ref, s_ref, kf_ref, bf_ref, r0, *, backward, fast):
    ch = GLA_CHUNK
    rows = slice(r0, r0 + ch)
    b = b_ref[rows, :]
    q = q_ref[rows, :].astype(F32)
    k = k_ref[rows, :].astype(F32)
    v = v_ref[rows, :]
    end = 0 if backward else ch - 1
    b_end = b[end:end + 1, :]
    qd = (q * jnp.exp(b)).astype(BF16)
    k_end = (k * jnp.exp(b_end - b)).astype(BF16)
    decay_end = jnp.exp(b_end)
    ri = lax.broadcasted_iota(jnp.int32, (ch, ch), 0)
    ci = lax.broadcasted_iota(jnp.int32, (ch, ch), 1)
    visible = (ci >= ri) if backward else (ci <= ri)
    if fast:
        kd = (k * jnp.exp(-b)).astype(BF16)
    else:
        kf_ref[...] = k
        bf_ref[...] = b
    for hd in range(GLA_HEADS):
        ksl = slice(hd * GLA_DK, (hd + 1) * GLA_DK)
        vsl = slice(hd * GLA_DV, (hd + 1) * GLA_DV)
        if fast:
            a = _dot_nt(qd[:, ksl], kd[:, ksl])
        else:
            q_h, b_h = q[:, ksl], b[:, ksl]

            def columns(grp, a_acc, q_h=q_h, b_h=b_h, ksl=ksl):
                base = pl.multiple_of(grp * SUBLANES, SUBLANES)
                k_rows = kf_ref[pl.ds(base, SUBLANES), ksl]
                b_rows = bf_ref[pl.ds(base, SUBLANES), ksl]
                for r in range(SUBLANES):
                    decay = jnp.exp(jnp.minimum(b_h - b_rows[r:r + 1, :], 0.0))
                    col = jnp.sum(q_h * k_rows[r:r + 1, :] * decay, axis=-1, keepdims=True)
                    a_acc = jnp.where(ci == base + r, col, a_acc)
                return a_acc

            a = lax.fori_loop(0, ch // SUBLANES, columns, jnp.zeros((ch, ch), F32))
        a = jnp.where(visible, a, 0.0).astype(BF16)
        st = s_ref[hd]
        o_ref[rows, vsl] = _dot_nt(qd[:, ksl], st.astype(BF16)) + _dot(a, v[:, vsl])
        s_ref[hd] = st * decay_end[:, ksl] + _dot_tn(v[:, vsl], k_end[:, ksl])


def _gla_kernel(qf_ref, kf_ref, vf_ref, bf_ref, qb_ref, kb_ref, vb_ref, bb_ref, of_ref, ob_ref,
                sf_ref, sb_ref, ktmp_ref, btmp_ref, *, tb):
    @pl.when(pl.program_id(1) == 0)
    def _():
        sf_ref[...] = jnp.zeros_like(sf_ref)
        sb_ref[...] = jnp.zeros_like(sb_ref)

    n_sub = tb // GLA_CHUNK
    both = jnp.minimum(bf_ref[...], bb_ref[...])
    min_b = jnp.min(jnp.min(both, axis=0, keepdims=True), axis=1, keepdims=True)[0, 0]

    def run(fast):
        for s in range(n_sub):
            _gla_chunk(qf_ref, kf_ref, vf_ref, bf_ref, of_ref, sf_ref, ktmp_ref, btmp_ref, s * GLA_CHUNK,
                       backward=False, fast=fast)
            _gla_chunk(qb_ref, kb_ref, vb_ref, bb_ref, ob_ref, sb_ref, ktmp_ref, btmp_ref,
                       (n_sub - 1 - s) * GLA_CHUNK, backward=True, fast=fast)

    pl.when(min_b >= GLA_SAFE_LOG_DECAY)(lambda: run(True))
    pl.when(min_b < GLA_SAFE_LOG_DECAY)(lambda: run(False))


def _gla(lay, gq, gk, gv, bdec):
    tb = lay.tb
    tl, ts = lay.l // tb, lay.s // tb
    lat0 = lay.n_ctx // tb

    def fwd_blk(b, t):
        return jnp.where(t < tl, b * tl + t, lat0 + b * ts + (t - tl))

    def bwd_blk(b, t):
        return jnp.where(t < tl, b * tl + (tl - 1 - t), lat0 + b * ts + (ts - 1 - (t - tl)))

    def specs(blk, half):
        return [
            pl.BlockSpec((tb, GLA_K_W), lambda b, t: (blk(b, t), 0)),
            pl.BlockSpec((tb, GLA_K_W), lambda b, t: (blk(b, t), 0)),
            pl.BlockSpec((tb, GLA_V_W), lambda b, t: (blk(b, t), 0)),
            pl.BlockSpec((tb, GLA_K_W), lambda b, t: (blk(b, t), half)),
        ]

    return pl.pallas_call(
        functools.partial(_gla_kernel, tb=tb),
        grid=(lay.b, tl + ts),
        in_specs=specs(fwd_blk, 0) + specs(bwd_blk, 1),
        out_specs=[
            pl.BlockSpec((tb, GLA_V_W), lambda b, t: (fwd_blk(b, t), 0)),
            pl.BlockSpec((tb, GLA_V_W), lambda b, t: (bwd_blk(b, t), 0)),
        ],
        out_shape=[jax.ShapeDtypeStruct((lay.n, GLA_V_W), F32)] * 2,
        scratch_shapes=[
            pltpu.VMEM((GLA_HEADS, GLA_DV, GLA_DK), F32),
            pltpu.VMEM((GLA_HEADS, GLA_DV, GLA_DK), F32),
            pltpu.VMEM((GLA_CHUNK, GLA_K_W), F32),
            pltpu.VMEM((GLA_CHUNK, GLA_K_W), F32),
        ],
        compiler_params=_params("arbitrary", "arbitrary"),
    )(gq, gk, gv, bdec, gq, gk, gv, bdec)


def _merge_kernel(x_ref, mod_ref, ya_ref, of_ref, ob_ref, sr_ref, sga_ref, sgg_ref, gnw_ref, wba_ref, wbg_ref, wo_ref,
                  o_ref):
    o = of_ref[...] + ob_ref[...]
    gnw = gnw_ref[...]
    parts = []
    for hd in range(GLA_HEADS):
        oh = o[:, hd * GLA_DV:(hd + 1) * GLA_DV]
        parts.append(oh * lax.rsqrt(jnp.mean(oh * oh, axis=-1, keepdims=True) + NORM_EPS) * gnw)
    y_gla = (jnp.concatenate(parts, axis=-1) * sr_ref[...].astype(F32)).astype(BF16)
    z = (sga_ref[...].astype(F32) * _dot(ya_ref[...], wba_ref[...])
         + sgg_ref[...].astype(F32) * _dot(y_gla, wbg_ref[...]))
    y = _dot(z.astype(BF16), wo_ref[...])
    o_ref[...] = x_ref[...] + _mod_slice(mod_ref, 5) * y


def _merge(lay, layer, x_all, mod, y_attn, o_f, o_b, sr, sga, sgg, gnw, wba, wbg, wo):
    tm = lay.tm
    row = pl.BlockSpec((tm, D_MODEL), lambda i: (i, 0))
    weight = _resident((None, D_MODEL, D_MODEL), lambda i: (layer, 0, 0))
    return pl.pallas_call(
        _merge_kernel,
        grid=(lay.tiles,),
        in_specs=[
            row,
            pl.BlockSpec((None, None, 1, N_MOD * D_MODEL), lambda i: (layer, lay.mod_row(i), 0, 0)),
            row, row, row, row, row, row,
            pl.BlockSpec((None, 1, GLA_DV), lambda i: (layer, 0, 0)),
            weight, weight, weight,
        ],
        out_specs=row,
        out_shape=jax.ShapeDtypeStruct((lay.n, D_MODEL), F32),
        compiler_params=_params("arbitrary"),
    )(x_all, mod, y_attn, o_f, o_b, sr, sga, sgg, gnw, wba, wbg, wo)


def kernel(x, c, ctx, c_ctx, w_mod, b_mod, norm_w, ffn1_w13, ffn1_w2, ffn2_w13, ffn2_w2, w_in, q_norm_w, k_norm_w,
           attn_sink, gla_gate_w_fwd, gla_gate_b_fwd, gla_gate_w_bwd, gla_gate_b_bwd, gla_norm_w, w_branch_attn,
           w_branch_gla, w_out):
    batch, seq, _ = x.shape
    ctx_len = ctx.shape[1]
    depth = w_mod.shape[0]
    assert batch + 1 <= MOD_ROWS
    lay = _Layout(batch, seq, ctx_len)

    cond_rows = jnp.concatenate([c, c_ctx[None, :], jnp.zeros((MOD_ROWS - batch - 1, D_MODEL), F32)], axis=0)
    ctx_cols = 2 * ATTN_KV_W + GLA_K_W + GLA_V_W
    lr0, lr1 = ctx_cols, ctx_cols + 2 * GLA_RANK
    wm = jnp.concatenate([w_in[:, :, :lr0], w_in[:, :, lr1:]], axis=-1).astype(BF16)
    wlr = jnp.pad(w_in[:, :, lr0:lr1], ((0, 0), (0, 0), (0, LR_PAD - 2 * GLA_RANK))).astype(BF16)
    gw = jnp.zeros((depth, LR_PAD, 2 * GLA_K_W), F32)
    gw = gw.at[:, :GLA_RANK, :GLA_K_W].set(gla_gate_w_fwd).at[:, GLA_RANK:2 * GLA_RANK, GLA_K_W:].set(gla_gate_w_bwd)
    gw = gw.astype(BF16)
    gb = jnp.concatenate([gla_gate_b_fwd, gla_gate_b_bwd], axis=-1)[:, None, :]
    w13_1, w2_1 = ffn1_w13.astype(BF16), ffn1_w2.astype(BF16)
    w13_2, w2_2 = ffn2_w13.astype(BF16), ffn2_w2.astype(BF16)
    wba, wbg, wo = w_branch_attn.astype(BF16), w_branch_gla.astype(BF16), w_out.astype(BF16)
    qnw, knw, gnw = q_norm_w[:, None, :], k_norm_w[:, None, :], gla_norm_w[:, None, :]
    rope_c, rope_s = _rope_tables(lay)

    mod = _modulation(cond_rows, w_mod, b_mod).reshape(depth, MOD_ROWS, 1, N_MOD * D_MODEL)
    x_all = jnp.concatenate([ctx.reshape(batch * ctx_len, D_MODEL), x.reshape(batch * seq, D_MODEL)], axis=0)

    for layer in range(depth):
        last = layer == depth - 1
        x_all = _ffn(lay, layer, 0, x_all, mod, norm_w, w13_1, w2_1)
        q, kv, gq, gk, gv, bdec, sr, sga, sgg = _proj(lay, layer, x_all, mod, norm_w, wm, wlr, gw, gb, qnw, knw,
                                                       rope_c, rope_s)
        y_attn = _attn(lay, layer, attn_sink, q, kv)
        o_f, o_b = _gla(lay, gq, gk, gv, bdec)
        x_all = _merge(lay, layer, x_all, mod, y_attn, o_f, o_b, sr, sga, sgg, gnw, wba, wbg, wo)
        x_all = _ffn(lay, layer, 2, x_all, mod, norm_w, w13_2, w2_2, latents_only=last)
    return x_all.reshape(batch, seq, D_MODEL)
```

```python
import functools

import numpy as np
import jax
import jax.numpy as jnp
from jax import lax
from jax.experimental import pallas as pl
from jax.experimental.pallas import tpu as pltpu

F32 = jnp.float32
BF16 = jnp.bfloat16

D_MODEL = 1024
N_MOD = 9
NORM_EPS = 1e-6
D_FF = 2816
GRID_W = 64
ATTN_HEADS = 8
ATTN_KV_HEADS = 2
ATTN_GROUP = ATTN_HEADS // ATTN_KV_HEADS
HEAD_DIM = 128
ATTN_BLOCK = 128
ROPE_THETA = 10000.0
LOG2_E = 1.4426950408889634
QK_SCALE_LOG2 = HEAD_DIM ** -0.5 * LOG2_E
GLA_HEADS = 4
GLA_DK = 128
GLA_DV = 256
GLA_RANK = 16
GLA_TEMP = 16.0
GLA_CHUNK = 128
ATTN_Q_W = ATTN_HEADS * HEAD_DIM
ATTN_KV_W = ATTN_KV_HEADS * HEAD_DIM
GLA_K_W = GLA_HEADS * GLA_DK
GLA_V_W = GLA_HEADS * GLA_DV
GLA_SAFE_LOG_DECAY = -60.0

V7X_VMEM_BYTES = 64 * 1024 * 1024
VMEM_LIMIT = V7X_VMEM_BYTES - 8 * 1024 * 1024
MOD_ROWS = 16
MOD_COL_TILE = 1024
SUBLANES = 8


def _dot(a, b):
    return jnp.dot(a, b, preferred_element_type=F32)


def _dot_nt(a, b):
    return lax.dot_general(a, b, (((1,), (1,)), ((), ())), preferred_element_type=F32)


def _dot_tn(a, b):
    return lax.dot_general(a, b, (((0,), (0,)), ((), ())), preferred_element_type=F32)


def _sigmoid(x):
    return 1.0 / (1.0 + jnp.exp(-x))


def _resident(block_shape, index_map):
    return pl.BlockSpec(block_shape, index_map, pipeline_mode=pl.Buffered(1))


def _params(*sem):
    return pltpu.CompilerParams(dimension_semantics=sem, vmem_limit_bytes=VMEM_LIMIT)


class _Layout:
    def __init__(self, batch, seq, ctx_len, tm_max=512):
        self.b, self.s, self.l = batch, seq, ctx_len
        self.n_ctx = batch * ctx_len
        self.n = self.n_ctx + batch * seq
        tm = tm_max
        while self.n_ctx % tm or seq % tm:
            tm //= 2
        assert tm >= GLA_CHUNK and tm % GLA_CHUNK == 0
        self.tm = tm
        self.ctx_tiles = self.n_ctx // tm
        self.tiles_per_batch = seq // tm
        self.tiles = self.n // tm
        tb = 256
        while ctx_len % tb or seq % tb:
            tb //= 2
        assert tb >= GLA_CHUNK
        self.tb = tb
        assert ctx_len % ATTN_BLOCK == 0 and seq % ATTN_BLOCK == 0 and seq % GRID_W == 0

    def mod_row(self, i):
        return jnp.where(i < self.ctx_tiles, self.b, (i - self.ctx_tiles) // self.tiles_per_batch)

    def rope_block(self, i):
        return jnp.where(i < self.ctx_tiles, 0, 1 + (i - self.ctx_tiles) % self.tiles_per_batch)


def _mod_kernel(c_ref, w_ref, b_ref, o_ref):
    c = c_ref[...]
    cond = (c * _sigmoid(c)).astype(BF16)
    o_ref[...] = _dot(cond, w_ref[...].astype(BF16)) + b_ref[...]


def _modulation(cond_rows, w_mod, b_mod):
    depth = w_mod.shape[0]
    n_cols = N_MOD * D_MODEL
    return pl.pallas_call(
        _mod_kernel,
        grid=(depth, n_cols // MOD_COL_TILE),
        in_specs=[
            pl.BlockSpec((MOD_ROWS, D_MODEL), lambda l, j: (0, 0)),
            pl.BlockSpec((None, D_MODEL, MOD_COL_TILE), lambda l, j: (l, 0, j)),
            pl.BlockSpec((None, 1, MOD_COL_TILE), lambda l, j: (l, 0, j)),
        ],
        out_specs=pl.BlockSpec((None, MOD_ROWS, MOD_COL_TILE), lambda l, j: (l, 0, j)),
        out_shape=jax.ShapeDtypeStruct((depth, MOD_ROWS, n_cols), F32),
        compiler_params=_params("arbitrary", "arbitrary"),
    )(cond_rows, w_mod, b_mod.reshape(depth, 1, n_cols))


def _mod_slice(mod_ref, k):
    return mod_ref[:, k * D_MODEL:(k + 1) * D_MODEL]


def _norm_modulate(x, nw, shift, scale):
    y = x * lax.rsqrt(jnp.mean(x * x, axis=-1, keepdims=True) + NORM_EPS) * nw
    return y * (1.0 + scale) + shift


FFN_CHUNKS = 11
FFN_TM_MAX = 1024


def _ffn_kernel(x_ref, mod_ref, nw_ref, w13_ref, w2_ref, o_ref, *, sub):
    x = x_ref[...]
    h = _norm_modulate(x, nw_ref[sub:sub + 1, :], _mod_slice(mod_ref, 3 * sub), _mod_slice(mod_ref, 3 * sub + 1))
    hb = h.astype(BF16)
    fc = D_FF // FFN_CHUNKS
    acc = None
    for c in range(FFN_CHUNKS):
        up = _dot(hb, w13_ref[:, c * fc:(c + 1) * fc])
        gate = _dot(hb, w13_ref[:, D_FF + c * fc:D_FF + (c + 1) * fc])
        a = (gate * _sigmoid(gate) * up).astype(BF16)
        part = _dot(a, w2_ref[c * fc:(c + 1) * fc, :])
        acc = part if acc is None else acc + part
    o_ref[...] = x + (0.5 * _mod_slice(mod_ref, 3 * sub + 2)) * acc


def _ffn(lay, layer, sub, x_all, mod, norm_w, w13, w2, *, latents_only=False):
    tm = lay.tm
    off = lay.ctx_tiles if latents_only else 0
    n_tiles = lay.tiles - off
    return pl.pallas_call(
        functools.partial(_ffn_kernel, sub=sub),
        grid=(n_tiles,),
        in_specs=[
            pl.BlockSpec((tm, D_MODEL), lambda i: (i + off, 0)),
            pl.BlockSpec((None, None, 1, N_MOD * D_MODEL), lambda i: (layer, lay.mod_row(i + off), 0, 0)),
            pl.BlockSpec((None, 3, D_MODEL), lambda i: (layer, 0, 0)),
            _resident((None, D_MODEL, 2 * D_FF), lambda i: (layer, 0, 0)),
            _resident((None, D_FF, D_MODEL), lambda i: (layer, 0, 0)),
        ],
        out_specs=pl.BlockSpec((tm, D_MODEL), lambda i: (i, 0)),
        out_shape=jax.ShapeDtypeStruct((n_tiles * tm, D_MODEL), F32),
        compiler_params=_params("arbitrary"),
    )(x_all, mod, norm_w, w13, w2)


C_AK, C_AV, C_GK, C_GV, C_AQ, C_GQ, C_GR, C_GA, C_GG, C_END = np.cumsum(
    [0, ATTN_KV_W, ATTN_KV_W, GLA_K_W, GLA_V_W, ATTN_Q_W, GLA_K_W, GLA_V_W, D_MODEL, D_MODEL]).tolist()
LR_PAD = 128
CUMSUM_ROWS = 256


def _log_sigmoid(x):
    return jnp.minimum(x, 0.0) - jnp.log(1.0 + jnp.exp(-jnp.abs(x)))


def _proj_kernel(x_ref, mod_ref, nw_ref, wm_ref, wlr_ref, gw_ref, gb_ref, qnw_ref, knw_ref, rc_ref, rs_ref,
                 q_ref, kv_ref, gq_ref, gk_ref, gv_ref, bdec_ref, sr_ref, sga_ref, sgg_ref):
    x = x_ref[...]
    tm = x.shape[0]
    h = _norm_modulate(x, nw_ref[1:2, :], _mod_slice(mod_ref, 3), _mod_slice(mod_ref, 4))
    hb = h.astype(BF16)
    rope_c = rc_ref[...]
    rope_s = rs_ref[...]
    lane = lax.broadcasted_iota(jnp.int32, (tm, HEAD_DIM), 1)
    first_half = (lane % (HEAD_DIM // 2)) < (HEAD_DIM // 4)

    def norm_rope(t, w):
        y = t * lax.rsqrt(jnp.mean(t * t, axis=-1, keepdims=True) + NORM_EPS) * w
        partner = jnp.where(first_half, pltpu.roll(y, HEAD_DIM - HEAD_DIM // 4, 1), pltpu.roll(y, HEAD_DIM // 4, 1))
        return y * rope_c + partner * rope_s

    kvp = _dot(hb, wm_ref[:, C_AK:C_GK])
    knw = knw_ref[...]
    for i in range(ATTN_KV_HEADS):
        sl = slice(i * HEAD_DIM, (i + 1) * HEAD_DIM)
        kv_ref[:, sl] = norm_rope(kvp[:, sl], knw).astype(BF16)
    kv_ref[:, ATTN_KV_W:] = kvp[:, ATTN_KV_W:].astype(BF16)
    gk_ref[...] = _dot(hb, wm_ref[:, C_GK:C_GV]).astype(BF16)
    gv_ref[...] = _dot(hb, wm_ref[:, C_GV:C_AQ]).astype(BF16)
    qp = _dot(hb, wm_ref[:, C_AQ:C_GQ])
    qnw = qnw_ref[...]
    for i in range(ATTN_HEADS):
        qh = (norm_rope(qp[:, i * HEAD_DIM:(i + 1) * HEAD_DIM], qnw) * QK_SCALE_LOG2).astype(BF16)
        for blk in range(tm // ATTN_BLOCK):
            dst = (blk * ATTN_HEADS + i) * ATTN_BLOCK
            q_ref[dst:dst + ATTN_BLOCK, :] = qh[blk * ATTN_BLOCK:(blk + 1) * ATTN_BLOCK, :]
    gq_ref[...] = (_dot(hb, wm_ref[:, C_GQ:C_GR]) * (GLA_DK ** -0.5)).astype(BF16)
    r = _dot(hb, wm_ref[:, C_GR:C_GA])
    sr_ref[...] = (r * _sigmoid(r)).astype(BF16)
    sga_ref[...] = _sigmoid(_dot(hb, wm_ref[:, C_GA:C_GG])).astype(BF16)
    sgg_ref[...] = _sigmoid(_dot(hb, wm_ref[:, C_GG:C_END])).astype(BF16)

    lr = _dot(hb, wlr_ref[...]).astype(BF16)
    g = _log_sigmoid(_dot(lr, gw_ref[...]) + gb_ref[...]) * (1.0 / GLA_TEMP)
    cr = min(CUMSUM_ROWS, tm)
    ri = lax.broadcasted_iota(jnp.int32, (cr, cr), 0)
    ci = lax.broadcasted_iota(jnp.int32, (cr, cr), 1)
    same = (ri // GLA_CHUNK) == (ci // GLA_CHUNK)
    t_fwd = jnp.where(same & (ci <= ri), 1.0, 0.0).astype(BF16)
    t_bwd = jnp.where(same & (ci >= ri), 1.0, 0.0).astype(BF16)
    for blk in range(tm // cr):
        rows = slice(blk * cr, (blk + 1) * cr)
        for tri, cols in ((t_fwd, slice(0, GLA_K_W)), (t_bwd, slice(GLA_K_W, 2 * GLA_K_W))):
            gp = g[rows, cols]
            hi = gp.astype(BF16)
            lo = (gp - hi.astype(F32)).astype(BF16)
            bdec_ref[rows, cols] = _dot(tri, hi) + _dot(tri, lo)


def _proj(lay, layer, x_all, mod, norm_w, wm, wlr, gw, gb, qnw, knw, rope_c, rope_s):
    tm, n = lay.tm, lay.n
    row = lambda w: pl.BlockSpec((tm, w), lambda i: (i, 0))
    widths = [2 * ATTN_KV_W, GLA_K_W, GLA_K_W, GLA_V_W, 2 * GLA_K_W, GLA_V_W, D_MODEL, D_MODEL]
    dtypes = [BF16, BF16, BF16, BF16, F32, BF16, BF16, BF16]
    q_spec = pl.BlockSpec((tm * ATTN_HEADS, HEAD_DIM), lambda i: (i, 0))
    q_shape = jax.ShapeDtypeStruct((n * ATTN_HEADS, HEAD_DIM), BF16)
    return pl.pallas_call(
        _proj_kernel,
        grid=(lay.tiles,),
        in_specs=[
            row(D_MODEL),
            pl.BlockSpec((None, None, 1, N_MOD * D_MODEL), lambda i: (layer, lay.mod_row(i), 0, 0)),
            pl.BlockSpec((None, 3, D_MODEL), lambda i: (layer, 0, 0)),
            _resident((None, D_MODEL, C_END), lambda i: (layer, 0, 0)),
            _resident((None, D_MODEL, LR_PAD), lambda i: (layer, 0, 0)),
            _resident((None, LR_PAD, 2 * GLA_K_W), lambda i: (layer, 0, 0)),
            pl.BlockSpec((None, 1, 2 * GLA_K_W), lambda i: (layer, 0, 0)),
            pl.BlockSpec((None, 1, HEAD_DIM), lambda i: (layer, 0, 0)),
            pl.BlockSpec((None, 1, HEAD_DIM), lambda i: (layer, 0, 0)),
            pl.BlockSpec((tm, HEAD_DIM), lambda i: (lay.rope_block(i), 0)),
            pl.BlockSpec((tm, HEAD_DIM), lambda i: (lay.rope_block(i), 0)),
        ],
        out_specs=[q_spec] + [row(w) for w in widths],
        out_shape=[q_shape] + [jax.ShapeDtypeStruct((n, w), dt) for w, dt in zip(widths, dtypes)],
        compiler_params=_params("arbitrary"),
    )(x_all, mod, norm_w, wm, wlr, gw, gb, qnw, knw, rope_c, rope_s)


def _rope_tables(lay):
    pos = np.arange(lay.s)
    half = HEAD_DIM // 2
    inv_freq = ROPE_THETA ** (-np.arange(0, half, 2, dtype=np.float32) / half)
    inv_freq = jnp.asarray(inv_freq, F32)
    ang_r = jnp.asarray(pos // GRID_W, F32)[:, None] * inv_freq[None, :]
    ang_c = jnp.asarray(pos % GRID_W, F32)[:, None] * inv_freq[None, :]
    cos = jnp.concatenate([jnp.cos(ang_r)] * 2 + [jnp.cos(ang_c)] * 2, axis=-1)
    sin = jnp.concatenate([-jnp.sin(ang_r), jnp.sin(ang_r), -jnp.sin(ang_c), jnp.sin(ang_c)], axis=-1)
    cos = jnp.concatenate([jnp.ones((lay.tm, HEAD_DIM), F32), cos], axis=0)
    sin = jnp.concatenate([jnp.zeros((lay.tm, HEAD_DIM), F32), sin], axis=0)
    return cos, sin


NEG_BIG = -1e30


def _attn_kernel(sink_ref, q_ref, kp_ref, kc_ref, kn_ref, kx_ref, o_ref, kcat_ref, vcat_ref, *,
                 layer, ctx_blocks, lat_blocks, ctx_len):
    j = pl.program_id(1)
    jl = j - ctx_blocks
    qb = ATTN_BLOCK
    rows = ATTN_GROUP * qb
    win = 3 * qb
    grp = lax.broadcasted_iota(jnp.int32, (rows, 1), 0) // qb

    def sink_column(kvh):
        sink = jnp.zeros((rows, 1), F32)
        for g in range(ATTN_GROUP):
            sink = jnp.where(grp == g, sink_ref[layer, kvh * ATTN_GROUP + g] * LOG2_E, sink)
        return sink

    def finish(kvh, s, v):
        sink = sink_column(kvh)
        m = jnp.maximum(jnp.max(s, axis=-1, keepdims=True), sink)
        p = jnp.exp2(s - m)
        den = jnp.sum(p, axis=-1, keepdims=True) + jnp.exp2(sink - m)
        o = _dot(p.astype(BF16), v) / den
        for g in range(ATTN_GROUP):
            hd = kvh * ATTN_GROUP + g
            o_ref[:, hd * HEAD_DIM:(hd + 1) * HEAD_DIM] = o[g * qb:(g + 1) * qb].astype(o_ref.dtype)

    @pl.when(j < ctx_blocks)
    def _():
        for kvh in range(ATTN_KV_HEADS):
            q = q_ref[kvh * rows:(kvh + 1) * rows, :]
            k = kx_ref[:, kvh * HEAD_DIM:(kvh + 1) * HEAD_DIM]
            v = kx_ref[:, ATTN_KV_W + kvh * HEAD_DIM:ATTN_KV_W + (kvh + 1) * HEAD_DIM]
            finish(kvh, _dot_nt(q, k), v)

    @pl.when(j >= ctx_blocks)
    def _():
        row_i = lax.broadcasted_iota(jnp.int32, (rows, qb), 0) % qb
        col_i = lax.broadcasted_iota(jnp.int32, (rows, qb), 1)
        see_prev = (col_i >= row_i) & (jl > 0)
        see_next = (col_i <= row_i) & (jl < lat_blocks - 1)
        for kvh in range(ATTN_KV_HEADS):
            ksl = slice(kvh * HEAD_DIM, (kvh + 1) * HEAD_DIM)
            vsl = slice(ATTN_KV_W + kvh * HEAD_DIM, ATTN_KV_W + (kvh + 1) * HEAD_DIM)
            for idx, ref in enumerate((kp_ref, kc_ref, kn_ref)):
                kcat_ref[idx * qb:(idx + 1) * qb, :] = ref[:, ksl]
                vcat_ref[idx * qb:(idx + 1) * qb, :] = ref[:, vsl]
            kcat_ref[win:win + ctx_len, :] = kx_ref[:, ksl]
            vcat_ref[win:win + ctx_len, :] = kx_ref[:, vsl]
            s = _dot_nt(q_ref[kvh * rows:(kvh + 1) * rows, :], kcat_ref[...])
            s = jnp.concatenate([jnp.where(see_prev, s[:, :qb], NEG_BIG), s[:, qb:2 * qb],
                                 jnp.where(see_next, s[:, 2 * qb:win], NEG_BIG), s[:, win:]], axis=1)
            finish(kvh, s, vcat_ref[...])


def _attn(lay, layer, sink, q, kv):
    qb = ATTN_BLOCK
    cb, lb = lay.l // qb, lay.s // qb
    lat0 = lay.n_ctx // qb

    def q_blk(b, j):
        return jnp.where(j < cb, b * cb + j, lat0 + b * lb + (j - cb))

    def win_blk(delta):
        def f(b, j):
            return (lat0 + b * lb + jnp.clip(j - cb + delta, 0, lb - 1), 0)
        return f

    return pl.pallas_call(
        functools.partial(_attn_kernel, layer=layer, ctx_blocks=cb, lat_blocks=lb, ctx_len=lay.l),
        grid=(lay.b, cb + lb),
        in_specs=[
            pl.BlockSpec(memory_space=pltpu.SMEM),
            pl.BlockSpec((qb * ATTN_HEADS, HEAD_DIM), lambda b, j: (q_blk(b, j), 0)),
            pl.BlockSpec((qb, 2 * ATTN_KV_W), win_blk(-1)),
            pl.BlockSpec((qb, 2 * ATTN_KV_W), win_blk(0)),
            pl.BlockSpec((qb, 2 * ATTN_KV_W), win_blk(1)),
            pl.BlockSpec((lay.l, 2 * ATTN_KV_W), lambda b, j: (b, 0)),
        ],
        out_specs=pl.BlockSpec((qb, ATTN_Q_W), lambda b, j: (q_blk(b, j), 0)),
        out_shape=jax.ShapeDtypeStruct((lay.n, ATTN_Q_W), BF16),
        scratch_shapes=[
            pltpu.VMEM((3 * qb + lay.l, HEAD_DIM), BF16),
            pltpu.VMEM((3 * qb + lay.l, HEAD_DIM), BF16),
        ],
        compiler_params=_params("arbitrary", "arbitrary"),
    )(sink, q, kv, kv, kv, kv)


def _gla_chunk(q_ref, k_ref, v_ref, b_ref, o_ref, s_ref, kf_ref, bf_ref, r0, *, backward, fast):
    ch = GLA_CHUNK
    rows = slice(r0, r0 + ch)
    b = b_ref[rows, :]
    q = q_ref[rows, :].astype(F32)
    k = k_ref[rows, :].astype(F32)
    v = v_ref[rows, :]
    end = 0 if backward else ch - 1
    b_end = b[end:end + 1, :]
    qd = (q * jnp.exp(b)).astype(BF16)
    k_end = (k * jnp.exp(b_end - b)).astype(BF16)
    decay_end = jnp.broadcast_to(jnp.exp(b_end), (GLA_DK, GLA_K_W))
    ri = lax.broadcasted_iota(jnp.int32, (ch, ch), 0)
    ci = lax.broadcasted_iota(jnp.int32, (ch, ch), 1)
    visible = (ci >= ri) if backward else (ci <= ri)
    diag = lax.broadcasted_iota(jnp.int32, (GLA_DK, GLA_DK), 0) == lax.broadcasted_iota(jnp.int32, (GLA_DK, GLA_DK), 1)
    if fast:
        kd = (k * jnp.exp(-b)).astype(BF16)
    else:
        kf_ref[...] = k
        bf_ref[...] = b
    for hd in range(GLA_HEADS):
        ksl = slice(hd * GLA_DK, (hd + 1) * GLA_DK)
        vsl = slice(hd * GLA_DV, (hd + 1) * GLA_DV)
        if fast:
            a = _dot_nt(qd[:, ksl], kd[:, ksl])
        else:
            q_h, b_h = q[:, ksl], b[:, ksl]

            def columns(grp, a_acc, q_h=q_h, b_h=b_h, ksl=ksl):
                base = pl.multiple_of(grp * SUBLANES, SUBLANES)
                k_rows = kf_ref[pl.ds(base, SUBLANES), ksl]
                b_rows = bf_ref[pl.ds(base, SUBLANES), ksl]
                for r in range(SUBLANES):
                    decay = jnp.exp(jnp.minimum(b_h - b_rows[r:r + 1, :], 0.0))
                    col = jnp.sum(q_h * k_rows[r:r + 1, :] * decay, axis=-1, keepdims=True)
                    a_acc = jnp.where(ci == base + r, col, a_acc)
                return a_acc

            a = lax.fori_loop(0, ch // SUBLANES, columns, jnp.zeros((ch, ch), F32))
        a = jnp.where(visible, a, 0.0).astype(BF16)
        both = _dot(jnp.concatenate([a, k_end[:, ksl].T], axis=0), v[:, vsl])
        st = s_ref[hd]
        o_ref[rows, vsl] = _dot(qd[:, ksl], st.astype(BF16)) + both[:ch]
        decay_col = jnp.sum(jnp.where(diag, decay_end[:, ksl], 0.0), axis=-1, keepdims=True)
        s_ref[hd] = st * decay_col + both[ch:]


def _gla_kernel(qf_ref, kf_ref, vf_ref, bf_ref, qb_ref, kb_ref, vb_ref, bb_ref, of_ref, ob_ref,
                sf_ref, sb_ref, ktmp_ref, btmp_ref, *, tb):
    @pl.when(pl.program_id(1) == 0)
    def _():
        sf_ref[...] = jnp.zeros_like(sf_ref)
        sb_ref[...] = jnp.zeros_like(sb_ref)

    n_sub = tb // GLA_CHUNK
    both = jnp.minimum(bf_ref[...], bb_ref[...])
    min_b = jnp.min(jnp.min(both, axis=0, keepdims=True), axis=1, keepdims=True)[0, 0]

    def run(fast):
        for s in range(n_sub):
            _gla_chunk(qf_ref, kf_ref, vf_ref, bf_ref, of_ref, sf_ref, ktmp_ref, btmp_ref, s * GLA_CHUNK,
                       backward=False, fast=fast)
            _gla_chunk(qb_ref, kb_ref, vb_ref, bb_ref, ob_ref, sb_ref, ktmp_ref, btmp_ref,
                       (n_sub - 1 - s) * GLA_CHUNK, backward=True, fast=fast)

    pl.when(min_b >= GLA_SAFE_LOG_DECAY)(lambda: run(True))
    pl.when(min_b < GLA_SAFE_LOG_DECAY)(lambda: run(False))


def _gla(lay, gq, gk, gv, bdec):
    tb = lay.tb
    tl, ts = lay.l // tb, lay.s // tb
    lat0 = lay.n_ctx // tb

    def fwd_blk(b, t):
        return jnp.where(t < tl, b * tl + t, lat0 + b * ts + (t - tl))

    def bwd_blk(b, t):
        return jnp.where(t < tl, b * tl + (tl - 1 - t), lat0 + b * ts + (ts - 1 - (t - tl)))

    def specs(blk, half):
        return [
            pl.BlockSpec((tb, GLA_K_W), lambda b, t: (blk(b, t), 0)),
            pl.BlockSpec((tb, GLA_K_W), lambda b, t: (blk(b, t), 0)),
            pl.BlockSpec((tb, GLA_V_W), lambda b, t: (blk(b, t), 0)),
            pl.BlockSpec((tb, GLA_K_W), lambda b, t: (blk(b, t), half)),
        ]

    return pl.pallas_call(
        functools.partial(_gla_kernel, tb=tb),
        grid=(lay.b, tl + ts),
        in_specs=specs(fwd_blk, 0) + specs(bwd_blk, 1),
        out_specs=[
            pl.BlockSpec((tb, GLA_V_W), lambda b, t: (fwd_blk(b, t), 0)),
            pl.BlockSpec((tb, GLA_V_W), lambda b, t: (bwd_blk(b, t), 0)),
        ],
        out_shape=[jax.ShapeDtypeStruct((lay.n, GLA_V_W), F32)] * 2,
        scratch_shapes=[
            pltpu.VMEM((GLA_HEADS, GLA_DK, GLA_DV), F32),
            pltpu.VMEM((GLA_HEADS, GLA_DK, GLA_DV), F32),
            pltpu.VMEM((GLA_CHUNK, GLA_K_W), F32),
            pltpu.VMEM((GLA_CHUNK, GLA_K_W), F32),
        ],
        compiler_params=_params("arbitrary", "arbitrary"),
    )(gq, gk, gv, bdec, gq, gk, gv, bdec)


def _merge_kernel(x_ref, mod_ref, ya_ref, of_ref, ob_ref, sr_ref, sga_ref, sgg_ref, gnw_ref, wba_ref, wbg_ref, wo_ref,
                  o_ref):
    o = of_ref[...] + ob_ref[...]
    gnw = gnw_ref[...]
    parts = []
    for hd in range(GLA_HEADS):
        oh = o[:, hd * GLA_DV:(hd + 1) * GLA_DV]
        parts.append(oh * lax.rsqrt(jnp.mean(oh * oh, axis=-1, keepdims=True) + NORM_EPS) * gnw)
    y_gla = (jnp.concatenate(parts, axis=-1) * sr_ref[...].astype(F32)).astype(BF16)
    z = (sga_ref[...].astype(F32) * _dot(ya_ref[...], wba_ref[...])
         + sgg_ref[...].astype(F32) * _dot(y_gla, wbg_ref[...]))
    y = _dot(z.astype(BF16), wo_ref[...])
    o_ref[...] = x_ref[...] + _mod_slice(mod_ref, 5) * y


def _merge(lay, layer, x_all, mod, y_attn, o_f, o_b, sr, sga, sgg, gnw, wba, wbg, wo):
    tm = lay.tm
    row = pl.BlockSpec((tm, D_MODEL), lambda i: (i, 0))
    weight = _resident((None, D_MODEL, D_MODEL), lambda i: (layer, 0, 0))
    return pl.pallas_call(
        _merge_kernel,
        grid=(lay.tiles,),
        in_specs=[
            row,
            pl.BlockSpec((None, None, 1, N_MOD * D_MODEL), lambda i: (layer, lay.mod_row(i), 0, 0)),
            row, row, row, row, row, row,
            pl.BlockSpec((None, 1, GLA_DV), lambda i: (layer, 0, 0)),
            weight, weight, weight,
        ],
        out_specs=row,
        out_shape=jax.ShapeDtypeStruct((lay.n, D_MODEL), F32),
        compiler_params=_params("arbitrary"),
    )(x_all, mod, y_attn, o_f, o_b, sr, sga, sgg, gnw, wba, wbg, wo)


def kernel(x, c, ctx, c_ctx, w_mod, b_mod, norm_w, ffn1_w13, ffn1_w2, ffn2_w13, ffn2_w2, w_in, q_norm_w, k_norm_w,
           attn_sink, gla_gate_w_fwd, gla_gate_b_fwd, gla_gate_w_bwd, gla_gate_b_bwd, gla_norm_w, w_branch_attn,
           w_branch_gla, w_out):
    batch, seq, _ = x.shape
    ctx_len = ctx.shape[1]
    depth = w_mod.shape[0]
    assert batch + 1 <= MOD_ROWS
    lay = _Layout(batch, seq, ctx_len)
    lay_ffn = _Layout(batch, seq, ctx_len, FFN_TM_MAX)

    cond_rows = jnp.concatenate([c, c_ctx[None, :], jnp.zeros((MOD_ROWS - batch - 1, D_MODEL), F32)], axis=0)
    ctx_cols = 2 * ATTN_KV_W + GLA_K_W + GLA_V_W
    lr0, lr1 = ctx_cols, ctx_cols + 2 * GLA_RANK
    wm = jnp.concatenate([w_in[:, :, :lr0], w_in[:, :, lr1:]], axis=-1).astype(BF16)
    wlr = jnp.pad(w_in[:, :, lr0:lr1], ((0, 0), (0, 0), (0, LR_PAD - 2 * GLA_RANK))).astype(BF16)
    gw = jnp.zeros((depth, LR_PAD, 2 * GLA_K_W), F32)
    gw = gw.at[:, :GLA_RANK, :GLA_K_W].set(gla_gate_w_fwd).at[:, GLA_RANK:2 * GLA_RANK, GLA_K_W:].set(gla_gate_w_bwd)
    gw = gw.astype(BF16)
    gb = jnp.concatenate([gla_gate_b_fwd, gla_gate_b_bwd], axis=-1)[:, None, :]
    w13_1, w2_1 = ffn1_w13.astype(BF16), ffn1_w2.astype(BF16)
    w13_2, w2_2 = ffn2_w13.astype(BF16), ffn2_w2.astype(BF16)
    wba, wbg, wo = w_branch_attn.astype(BF16), w_branch_gla.astype(BF16), w_out.astype(BF16)
    qnw, knw, gnw = q_norm_w[:, None, :], k_norm_w[:, None, :], gla_norm_w[:, None, :]
    rope_c, rope_s = _rope_tables(lay)

    mod = _modulation(cond_rows, w_mod, b_mod).reshape(depth, MOD_ROWS, 1, N_MOD * D_MODEL)
    x_all = jnp.concatenate([ctx.reshape(batch * ctx_len, D_MODEL), x.reshape(batch * seq, D_MODEL)], axis=0)

    for layer in range(depth):
        last = layer == depth - 1
        x_all = _ffn(lay_ffn, layer, 0, x_all, mod, norm_w, w13_1, w2_1)
        q, kv, gq, gk, gv, bdec, sr, sga, sgg = _proj(lay, layer, x_all, mod, norm_w, wm, wlr, gw, gb, qnw, knw,
                                                       rope_c, rope_s)
        y_attn = _attn(lay, layer, attn_sink, q, kv)
        o_f, o_b = _gla(lay, gq, gk, gv, bdec)
        x_all = _merge(lay, layer, x_all, mod, y_attn, o_f, o_b, sr, sga, sgg, gnw, wba, wbg, wo)
        x_all = _ffn(lay_ffn, layer, 2, x_all, mod, norm_w, w13_2, w2_2, latents_only=last)
    return x_all.reshape(batch, seq, D_MODEL)
```

```python
import functools

import numpy as np
import jax
import jax.numpy as jnp
from jax import lax
from jax.experimental import pallas as pl
from jax.experimental.pallas import tpu as pltpu

F32 = jnp.float32
BF16 = jnp.bfloat16

D_MODEL = 1024
N_MOD = 9
NORM_EPS = 1e-6
D_FF = 2816
GRID_W = 64
ATTN_HEADS = 8
ATTN_KV_HEADS = 2
ATTN_GROUP = ATTN_HEADS // ATTN_KV_HEADS
HEAD_DIM = 128
ATTN_BLOCK = 128
ROPE_THETA = 10000.0
LOG2_E = 1.4426950408889634
QK_SCALE_LOG2 = HEAD_DIM ** -0.5 * LOG2_E
GLA_HEADS = 4
GLA_DK = 128
GLA_DV = 256
GLA_RANK = 16
GLA_TEMP = 16.0
GLA_CHUNK = 128
ATTN_Q_W = ATTN_HEADS * HEAD_DIM
ATTN_KV_W = ATTN_KV_HEADS * HEAD_DIM
GLA_K_W = GLA_HEADS * GLA_DK
GLA_V_W = GLA_HEADS * GLA_DV
GLA_SAFE_LOG_DECAY = -60.0

V7X_VMEM_BYTES = 64 * 1024 * 1024
VMEM_LIMIT = V7X_VMEM_BYTES - 8 * 1024 * 1024
MOD_ROWS = 16
MOD_COL_TILE = 1024
SUBLANES = 8
LANES = 128


def _dot(a, b):
    return jnp.dot(a, b, preferred_element_type=F32)


def _dot_nt(a, b):
    return lax.dot_general(a, b, (((1,), (1,)), ((), ())), preferred_element_type=F32)


def _dot_tn(a, b):
    return lax.dot_general(a, b, (((0,), (0,)), ((), ())), preferred_element_type=F32)


def _sigmoid(x):
    return 1.0 / (1.0 + jnp.exp(-x))


def _resident(block_shape, index_map):
    return pl.BlockSpec(block_shape, index_map, pipeline_mode=pl.Buffered(1))


def _params(*sem):
    return pltpu.CompilerParams(dimension_semantics=sem, vmem_limit_bytes=VMEM_LIMIT)


class _Layout:
    def __init__(self, batch, seq, ctx_len, tm_max=512):
        self.b, self.s, self.l = batch, seq, ctx_len
        self.n_ctx = batch * ctx_len
        self.n = self.n_ctx + batch * seq
        tm = tm_max
        while self.n_ctx % tm or seq % tm:
            tm //= 2
        assert tm >= GLA_CHUNK and tm % GLA_CHUNK == 0
        self.tm = tm
        self.ctx_tiles = self.n_ctx // tm
        self.tiles_per_batch = seq // tm
        self.tiles = self.n // tm
        tb = 256
        while ctx_len % tb or seq % tb:
            tb //= 2
        assert tb >= GLA_CHUNK
        self.tb = tb
        assert ctx_len % ATTN_BLOCK == 0 and seq % ATTN_BLOCK == 0 and seq % GRID_W == 0

    def mod_row(self, i):
        return jnp.where(i < self.ctx_tiles, self.b, (i - self.ctx_tiles) // self.tiles_per_batch)

    def rope_block(self, i):
        return jnp.where(i < self.ctx_tiles, 0, 1 + (i - self.ctx_tiles) % self.tiles_per_batch)


def _mod_kernel(c_ref, w_ref, b_ref, o_ref):
    c = c_ref[...]
    cond = (c * _sigmoid(c)).astype(BF16)
    o_ref[...] = _dot(cond, w_ref[...].astype(BF16)) + b_ref[...]


def _modulation(cond_rows, w_mod, b_mod):
    depth = w_mod.shape[0]
    n_cols = N_MOD * D_MODEL
    return pl.pallas_call(
        _mod_kernel,
        grid=(depth, n_cols // MOD_COL_TILE),
        in_specs=[
            pl.BlockSpec((MOD_ROWS, D_MODEL), lambda l, j: (0, 0)),
            pl.BlockSpec((None, D_MODEL, MOD_COL_TILE), lambda l, j: (l, 0, j)),
            pl.BlockSpec((None, 1, MOD_COL_TILE), lambda l, j: (l, 0, j)),
        ],
        out_specs=pl.BlockSpec((None, MOD_ROWS, MOD_COL_TILE), lambda l, j: (l, 0, j)),
        out_shape=jax.ShapeDtypeStruct((depth, MOD_ROWS, n_cols), F32),
        compiler_params=_params("arbitrary", "arbitrary"),
    )(cond_rows, w_mod, b_mod.reshape(depth, 1, n_cols))


def _mod_slice(mod_ref, k):
    return mod_ref[:, k * D_MODEL:(k + 1) * D_MODEL]


def _norm_modulate(x, nw, shift, scale):
    y = x * lax.rsqrt(jnp.mean(x * x, axis=-1, keepdims=True) + NORM_EPS) * nw
    return y * (1.0 + scale) + shift


FFN_CHUNKS = 11
FFN_TM_MAX = 1024


def _ffn_kernel(x_ref, mod_ref, nw_ref, w13_ref, w2_ref, o_ref, *, sub):
    _ffn_body(x_ref[...], mod_ref, nw_ref, w13_ref, w2_ref, o_ref, sub)


def _ffn_first_kernel(xc_ref, xl_ref, mod_ref, nw_ref, w13_ref, w2_ref, o_ref, *, sub, ctx_tiles):
    x = jnp.where(pl.program_id(0) < ctx_tiles, xc_ref[...], xl_ref[...])
    _ffn_body(x, mod_ref, nw_ref, w13_ref, w2_ref, o_ref, sub)


def _ffn_body(x, mod_ref, nw_ref, w13_ref, w2_ref, o_ref, sub):
    h = _norm_modulate(x, nw_ref[sub:sub + 1, :], _mod_slice(mod_ref, 3 * sub), _mod_slice(mod_ref, 3 * sub + 1))
    hb = h.astype(BF16)
    fc = D_FF // FFN_CHUNKS
    acc = None
    for c in range(FFN_CHUNKS):
        up = _dot(hb, w13_ref[:, c * fc:(c + 1) * fc])
        gate = _dot(hb, w13_ref[:, D_FF + c * fc:D_FF + (c + 1) * fc])
        a = (gate * _sigmoid(gate) * up).astype(BF16)
        part = _dot(a, w2_ref[c * fc:(c + 1) * fc, :])
        acc = part if acc is None else acc + part
    o_ref[...] = x + (0.5 * _mod_slice(mod_ref, 3 * sub + 2)) * acc


def _ffn(lay, layer, sub, x_rows, mod, norm_w, w13, w2, *, latents_only=False):
    tm = lay.tm
    off = lay.ctx_tiles if latents_only else 0
    n_tiles = lay.tiles - off
    if isinstance(x_rows, tuple):
        assert not latents_only
        body = functools.partial(_ffn_first_kernel, sub=sub, ctx_tiles=lay.ctx_tiles)
        row_specs = [
            pl.BlockSpec((tm, D_MODEL), lambda i: (jnp.minimum(i, lay.ctx_tiles - 1), 0)),
            pl.BlockSpec((tm, D_MODEL), lambda i: (jnp.maximum(i - lay.ctx_tiles, 0), 0)),
        ]
    else:
        body = functools.partial(_ffn_kernel, sub=sub)
        row_specs = [pl.BlockSpec((tm, D_MODEL), lambda i: (i + off, 0))]
        x_rows = (x_rows,)
    return pl.pallas_call(
        body,
        grid=(n_tiles,),
        in_specs=row_specs + [
            pl.BlockSpec((None, None, 1, N_MOD * D_MODEL), lambda i: (layer, lay.mod_row(i + off), 0, 0)),
            pl.BlockSpec((None, 3, D_MODEL), lambda i: (layer, 0, 0)),
            _resident((None, D_MODEL, 2 * D_FF), lambda i: (layer, 0, 0)),
            _resident((None, D_FF, D_MODEL), lambda i: (layer, 0, 0)),
        ],
        out_specs=pl.BlockSpec((tm, D_MODEL), lambda i: (i, 0)),
        out_shape=jax.ShapeDtypeStruct((n_tiles * tm, D_MODEL), F32),
        compiler_params=_params("arbitrary"),
    )(*x_rows, mod, norm_w, w13, w2)


C_AK, C_AV, C_GK, C_GV, C_AQ, C_GQ, C_GR, C_GA, C_GG, C_END = np.cumsum(
    [0, ATTN_KV_W, ATTN_KV_W, GLA_K_W, GLA_V_W, ATTN_Q_W, GLA_K_W, GLA_V_W, D_MODEL, D_MODEL]).tolist()
LR_PAD = 128
CUMSUM_ROWS = 256
PROJ_COL_CHUNK = 256


def _log_sigmoid(x):
    return jnp.minimum(x, 0.0) - jnp.log(1.0 + jnp.exp(-jnp.abs(x)))


def _proj_kernel(x_ref, mod_ref, nw_ref, wm_ref, wlr_ref, gw_ref, gb_ref, qnw_ref, knw_ref, rc_ref, rs_ref,
                 q_ref, kv_ref, gq_ref, gk_ref, gv_ref, bdec_ref, sr_ref, sga_ref, sgg_ref, bmin_ref):
    x = x_ref[...]
    tm = x.shape[0]
    h = _norm_modulate(x, nw_ref[1:2, :], _mod_slice(mod_ref, 3), _mod_slice(mod_ref, 4))
    hb = h.astype(BF16)
    rope_c = rc_ref[...]
    rope_s = rs_ref[...]
    lane = lax.broadcasted_iota(jnp.int32, (tm, HEAD_DIM), 1)
    first_half = (lane % (HEAD_DIM // 2)) < (HEAD_DIM // 4)

    def norm_rope(t, w):
        y = t * lax.rsqrt(jnp.mean(t * t, axis=-1, keepdims=True) + NORM_EPS) * w
        partner = jnp.where(first_half, pltpu.roll(y, HEAD_DIM - HEAD_DIM // 4, 1), pltpu.roll(y, HEAD_DIM // 4, 1))
        return y * rope_c + partner * rope_s

    def decay_gates():
        lr = _dot(hb, wlr_ref[...]).astype(BF16)
        g = _log_sigmoid(_dot(lr, gw_ref[...]) + gb_ref[...]) * (1.0 / GLA_TEMP)
        cr = min(CUMSUM_ROWS, tm)
        ri = lax.broadcasted_iota(jnp.int32, (cr, cr), 0)
        ci = lax.broadcasted_iota(jnp.int32, (cr, cr), 1)
        same = (ri // GLA_CHUNK) == (ci // GLA_CHUNK)
        t_fwd = jnp.where(same & (ci <= ri), 1.0, 0.0).astype(BF16)
        t_bwd = jnp.where(same & (ci >= ri), 1.0, 0.0).astype(BF16)
        b_min = None
        for blk in range(tm // cr):
            rows = slice(blk * cr, (blk + 1) * cr)
            for tri, cols in ((t_fwd, slice(0, GLA_K_W)), (t_bwd, slice(GLA_K_W, 2 * GLA_K_W))):
                gp = g[rows, cols]
                hi = gp.astype(BF16)
                lo = (gp - hi.astype(F32)).astype(BF16)
                b = _dot(tri, hi) + _dot(tri, lo)
                bdec_ref[rows, cols] = b
                b_min = b if b_min is None else jnp.minimum(b_min, b)
        b_min = jnp.min(jnp.min(b_min, axis=0, keepdims=True), axis=1, keepdims=True)
        bmin_ref[...] = jnp.broadcast_to(b_min, bmin_ref.shape)

    def project(c0, c1, out_ref, finish):
        for a in range(c0, c1, PROJ_COL_CHUNK):
            out_ref[:, a - c0:a - c0 + PROJ_COL_CHUNK] = finish(_dot(hb, wm_ref[:, a:a + PROJ_COL_CHUNK])).astype(BF16)

    decay_gates()
    kp = _dot(hb, wm_ref[:, C_AK:C_AV])
    knw = knw_ref[...]
    for i in range(ATTN_KV_HEADS):
        sl = slice(i * HEAD_DIM, (i + 1) * HEAD_DIM)
        kv_ref[:, sl] = norm_rope(kp[:, sl], knw).astype(BF16)
    kv_ref[:, ATTN_KV_W:] = _dot(hb, wm_ref[:, C_AV:C_GK]).astype(BF16)
    project(C_GK, C_GV, gk_ref, lambda t: t)
    project(C_GV, C_AQ, gv_ref, lambda t: t)
    qnw = qnw_ref[...]
    heads_per_chunk = PROJ_COL_CHUNK // HEAD_DIM
    for c in range(ATTN_HEADS // heads_per_chunk):
        qp = _dot(hb, wm_ref[:, C_AQ + c * PROJ_COL_CHUNK:C_AQ + (c + 1) * PROJ_COL_CHUNK])
        for ih in range(heads_per_chunk):
            i = c * heads_per_chunk + ih
            qh = (norm_rope(qp[:, ih * HEAD_DIM:(ih + 1) * HEAD_DIM], qnw) * QK_SCALE_LOG2).astype(BF16)
            for blk in range(tm // ATTN_BLOCK):
                dst = (blk * ATTN_HEADS + i) * ATTN_BLOCK
                q_ref[dst:dst + ATTN_BLOCK, :] = qh[blk * ATTN_BLOCK:(blk + 1) * ATTN_BLOCK, :]
    project(C_GQ, C_GR, gq_ref, lambda t: t * (GLA_DK ** -0.5))
    project(C_GR, C_GA, sr_ref, lambda t: t * _sigmoid(t))
    project(C_GA, C_GG, sga_ref, _sigmoid)
    project(C_GG, C_END, sgg_ref, _sigmoid)


def _proj(lay, layer, x_all, mod, norm_w, wm, wlr, gw, gb, qnw, knw, rope_c, rope_s):
    tm, n = lay.tm, lay.n
    row = lambda w: pl.BlockSpec((tm, w), lambda i: (i, 0))
    widths = [2 * ATTN_KV_W, GLA_K_W, GLA_K_W, GLA_V_W, 2 * GLA_K_W, GLA_V_W, D_MODEL, D_MODEL]
    dtypes = [BF16, BF16, BF16, BF16, F32, BF16, BF16, BF16]
    q_spec = pl.BlockSpec((tm * ATTN_HEADS, HEAD_DIM), lambda i: (i, 0))
    q_shape = jax.ShapeDtypeStruct((n * ATTN_HEADS, HEAD_DIM), BF16)
    return pl.pallas_call(
        _proj_kernel,
        grid=(lay.tiles,),
        in_specs=[
            row(D_MODEL),
            pl.BlockSpec((None, None, 1, N_MOD * D_MODEL), lambda i: (layer, lay.mod_row(i), 0, 0)),
            pl.BlockSpec((None, 3, D_MODEL), lambda i: (layer, 0, 0)),
            _resident((None, D_MODEL, C_END), lambda i: (layer, 0, 0)),
            _resident((None, D_MODEL, LR_PAD), lambda i: (layer, 0, 0)),
            _resident((None, LR_PAD, 2 * GLA_K_W), lambda i: (layer, 0, 0)),
            pl.BlockSpec((None, 1, 2 * GLA_K_W), lambda i: (layer, 0, 0)),
            pl.BlockSpec((None, 1, HEAD_DIM), lambda i: (layer, 0, 0)),
            pl.BlockSpec((None, 1, HEAD_DIM), lambda i: (layer, 0, 0)),
            pl.BlockSpec((tm, HEAD_DIM), lambda i: (lay.rope_block(i), 0)),
            pl.BlockSpec((tm, HEAD_DIM), lambda i: (lay.rope_block(i), 0)),
        ],
        out_specs=[q_spec] + [row(w) for w in widths] + [pl.BlockSpec((SUBLANES, LANES), lambda i: (i, 0))],
        out_shape=([q_shape] + [jax.ShapeDtypeStruct((n, w), dt) for w, dt in zip(widths, dtypes)]
                   + [jax.ShapeDtypeStruct((lay.tiles * SUBLANES, LANES), F32)]),
        compiler_params=_params("arbitrary"),
    )(x_all, mod, norm_w, wm, wlr, gw, gb, qnw, knw, rope_c, rope_s)


def _rope_tables(lay):
    pos = np.arange(lay.s)
    half = HEAD_DIM // 2
    inv_freq = ROPE_THETA ** (-np.arange(0, half, 2, dtype=np.float32) / half)
    inv_freq = jnp.asarray(inv_freq, F32)
    ang_r = jnp.asarray(pos // GRID_W, F32)[:, None] * inv_freq[None, :]
    ang_c = jnp.asarray(pos % GRID_W, F32)[:, None] * inv_freq[None, :]
    cos = jnp.concatenate([jnp.cos(ang_r)] * 2 + [jnp.cos(ang_c)] * 2, axis=-1)
    sin = jnp.concatenate([-jnp.sin(ang_r), jnp.sin(ang_r), -jnp.sin(ang_c), jnp.sin(ang_c)], axis=-1)
    cos = jnp.concatenate([jnp.ones((lay.tm, HEAD_DIM), F32), cos], axis=0)
    sin = jnp.concatenate([jnp.zeros((lay.tm, HEAD_DIM), F32), sin], axis=0)
    return cos, sin


NEG_BIG = -1e30
ATTN_FIXED_SHIFT_MAX = 50.0
ATTN_BLOCKS_PER_STEP = 2
ATTN_BOUND_SLACK = 1.02


def _attn_kernel(sink_ref, bound_ref, q_ref, kp_ref, kc_ref, kn_ref, kx_ref, o_ref, kcat_ref, vcat_ref, *,
                 layer, ctx_steps, lat_steps, ctx_len, nb):
    j = pl.program_id(1)
    jl = j - ctx_steps
    qb = ATTN_BLOCK
    rows = ATTN_GROUP * qb
    win = 3 * qb
    grp = lax.broadcasted_iota(jnp.int32, (rows, 1), 0) // qb

    def sink_column(kvh):
        sink = jnp.zeros((rows, 1), F32)
        for g in range(ATTN_GROUP):
            sink = jnp.where(grp == g, sink_ref[layer, kvh * ATTN_GROUP + g] * LOG2_E, sink)
        return sink

    bound = bound_ref[layer]
    small = bound <= ATTN_FIXED_SHIFT_MAX

    def q_rows(u, kvh):
        start = (u * ATTN_KV_HEADS + kvh) * rows
        return q_ref[start:start + rows, :]

    def finish(u, kvh, s, v, fixed_shift):
        sink = sink_column(kvh)
        m = jnp.maximum(bound, sink) if fixed_shift else jnp.maximum(jnp.max(s, axis=-1, keepdims=True), sink)
        p = jnp.exp2(s - m)
        den = jnp.sum(p, axis=-1, keepdims=True) + jnp.exp2(sink - m)
        o = _dot(p.astype(BF16), v) / den
        for g in range(ATTN_GROUP):
            hd = kvh * ATTN_GROUP + g
            o_ref[u * qb:(u + 1) * qb, hd * HEAD_DIM:(hd + 1) * HEAD_DIM] = o[g * qb:(g + 1) * qb].astype(o_ref.dtype)

    def context_queries(fixed_shift):
        for u in range(nb):
            for kvh in range(ATTN_KV_HEADS):
                k = kx_ref[:, kvh * HEAD_DIM:(kvh + 1) * HEAD_DIM]
                v = kx_ref[:, ATTN_KV_W + kvh * HEAD_DIM:ATTN_KV_W + (kvh + 1) * HEAD_DIM]
                finish(u, kvh, _dot_nt(q_rows(u, kvh), k), v, fixed_shift)

    def window_piece(p):
        if p < 0:
            return kp_ref, (nb + p) * qb, jl > 0
        if p >= nb:
            return kn_ref, (p - nb) * qb, jl < lat_steps - 1
        return kc_ref, p * qb, None

    def latent_queries(fixed_shift):
        row_i = lax.broadcasted_iota(jnp.int32, (rows, qb), 0) % qb
        col_i = lax.broadcasted_iota(jnp.int32, (rows, qb), 1)
        for u in range(nb):
            pieces = [window_piece(u + d) for d in (-1, 0, 1)]
            see_prev = col_i >= row_i if pieces[0][2] is None else (col_i >= row_i) & pieces[0][2]
            see_next = col_i <= row_i if pieces[2][2] is None else (col_i <= row_i) & pieces[2][2]
            for kvh in range(ATTN_KV_HEADS):
                ksl = slice(kvh * HEAD_DIM, (kvh + 1) * HEAD_DIM)
                vsl = slice(ATTN_KV_W + kvh * HEAD_DIM, ATTN_KV_W + (kvh + 1) * HEAD_DIM)
                for idx, (ref, r0, _) in enumerate(pieces):
                    kcat_ref[idx * qb:(idx + 1) * qb, :] = ref[r0:r0 + qb, ksl]
                    vcat_ref[idx * qb:(idx + 1) * qb, :] = ref[r0:r0 + qb, vsl]
                kcat_ref[win:win + ctx_len, :] = kx_ref[:, ksl]
                vcat_ref[win:win + ctx_len, :] = kx_ref[:, vsl]
                s = _dot_nt(q_rows(u, kvh), kcat_ref[...])
                s = jnp.concatenate([jnp.where(see_prev, s[:, :qb], NEG_BIG), s[:, qb:2 * qb],
                                     jnp.where(see_next, s[:, 2 * qb:win], NEG_BIG), s[:, win:]], axis=1)
                finish(u, kvh, s, vcat_ref[...], fixed_shift)

    is_ctx = j < ctx_steps
    pl.when(is_ctx & small)(lambda: context_queries(True))
    pl.when(is_ctx & jnp.logical_not(small))(lambda: context_queries(False))
    pl.when(jnp.logical_not(is_ctx) & small)(lambda: latent_queries(True))
    pl.when(jnp.logical_not(is_ctx) & jnp.logical_not(small))(lambda: latent_queries(False))


def _attn(lay, layer, sink, bound, q, kv):
    nb = ATTN_BLOCKS_PER_STEP
    while lay.l % (nb * ATTN_BLOCK) or lay.s % (nb * ATTN_BLOCK):
        nb //= 2
    qb = ATTN_BLOCK
    span = nb * qb
    cb, lb = lay.l // span, lay.s // span
    lat0 = lay.n_ctx // span

    def q_blk(b, j):
        return jnp.where(j < cb, b * cb + j, lat0 + b * lb + (j - cb))

    def win_blk(delta):
        def f(b, j):
            return (lat0 + b * lb + jnp.clip(j - cb + delta, 0, lb - 1), 0)
        return f

    return pl.pallas_call(
        functools.partial(_attn_kernel, layer=layer, ctx_steps=cb, lat_steps=lb, ctx_len=lay.l, nb=nb),
        grid=(lay.b, cb + lb),
        in_specs=[
            pl.BlockSpec(memory_space=pltpu.SMEM),
            pl.BlockSpec(memory_space=pltpu.SMEM),
            pl.BlockSpec((span * ATTN_HEADS, HEAD_DIM), lambda b, j: (q_blk(b, j), 0)),
            pl.BlockSpec((span, 2 * ATTN_KV_W), win_blk(-1)),
            pl.BlockSpec((span, 2 * ATTN_KV_W), win_blk(0)),
            pl.BlockSpec((span, 2 * ATTN_KV_W), win_blk(1)),
            pl.BlockSpec((lay.l, 2 * ATTN_KV_W), lambda b, j: (b, 0)),
        ],
        out_specs=pl.BlockSpec((span, ATTN_Q_W), lambda b, j: (q_blk(b, j), 0)),
        out_shape=jax.ShapeDtypeStruct((lay.n, ATTN_Q_W), BF16),
        scratch_shapes=[
            pltpu.VMEM((3 * qb + lay.l, HEAD_DIM), BF16),
            pltpu.VMEM((3 * qb + lay.l, HEAD_DIM), BF16),
        ],
        compiler_params=_params("arbitrary", "arbitrary"),
    )(sink, bound, q, kv, kv, kv, kv)


def _gla_chunk(q_ref, k_ref, v_ref, b_ref, o_ref, s_ref, kf_ref, bf_ref, r0, *, backward, fast):
    ch = GLA_CHUNK
    rows = slice(r0, r0 + ch)
    b = b_ref[rows, :]
    q = q_ref[rows, :].astype(F32)
    k = k_ref[rows, :].astype(F32)
    v = v_ref[rows, :]
    end = 0 if backward else ch - 1
    b_end = b[end:end + 1, :]
    qd = (q * jnp.exp(b)).astype(BF16)
    k_end = (k * jnp.exp(b_end - b)).astype(BF16)
    decay_end = jnp.broadcast_to(jnp.exp(b_end), (GLA_DK, GLA_K_W))
    ri = lax.broadcasted_iota(jnp.int32, (ch, ch), 0)
    ci = lax.broadcasted_iota(jnp.int32, (ch, ch), 1)
    visible = (ci >= ri) if backward else (ci <= ri)
    diag = lax.broadcasted_iota(jnp.int32, (GLA_DK, GLA_DK), 0) == lax.broadcasted_iota(jnp.int32, (GLA_DK, GLA_DK), 1)
    if fast:
        kd = (k * jnp.exp(-b)).astype(BF16)
    else:
        kf_ref[...] = k
        bf_ref[...] = b
    for hd in range(GLA_HEADS):
        ksl = slice(hd * GLA_DK, (hd + 1) * GLA_DK)
        vsl = slice(hd * GLA_DV, (hd + 1) * GLA_DV)
        if fast:
            a = _dot_nt(qd[:, ksl], kd[:, ksl])
        else:
            q_h, b_h = q[:, ksl], b[:, ksl]

            def columns(grp, a_acc, q_h=q_h, b_h=b_h, ksl=ksl):
                base = pl.multiple_of(grp * SUBLANES, SUBLANES)
                k_rows = kf_ref[pl.ds(base, SUBLANES), ksl]
                b_rows = bf_ref[pl.ds(base, SUBLANES), ksl]
                for r in range(SUBLANES):
                    decay = jnp.exp(jnp.minimum(b_h - b_rows[r:r + 1, :], 0.0))
                    col = jnp.sum(q_h * k_rows[r:r + 1, :] * decay, axis=-1, keepdims=True)
                    a_acc = jnp.where(ci == base + r, col, a_acc)
                return a_acc

            a = lax.fori_loop(0, ch // SUBLANES, columns, jnp.zeros((ch, ch), F32))
        a = jnp.where(visible, a, 0.0).astype(BF16)
        both = _dot(jnp.concatenate([a, k_end[:, ksl].T], axis=0), v[:, vsl])
        st = s_ref[hd]
        o_ref[rows, vsl] = _dot(qd[:, ksl], st.astype(BF16)) + both[:ch]
        decay_col = jnp.sum(jnp.where(diag, decay_end[:, ksl], 0.0), axis=-1, keepdims=True)
        s_ref[hd] = st * decay_col + both[ch:]


def _gla_kernel(qf_ref, kf_ref, vf_ref, bf_ref, mf_ref, qb_ref, kb_ref, vb_ref, bb_ref, mb_ref, of_ref, ob_ref,
                sf_ref, sb_ref, ktmp_ref, btmp_ref, *, tb):
    @pl.when(pl.program_id(1) == 0)
    def _():
        sf_ref[...] = jnp.zeros_like(sf_ref)
        sb_ref[...] = jnp.zeros_like(sb_ref)

    n_sub = tb // GLA_CHUNK
    both = jnp.minimum(mf_ref[...], mb_ref[...])
    min_b = jnp.min(jnp.min(both, axis=0, keepdims=True), axis=1, keepdims=True)[0, 0]

    def run(fast):
        for s in range(n_sub):
            _gla_chunk(qf_ref, kf_ref, vf_ref, bf_ref, of_ref, sf_ref, ktmp_ref, btmp_ref, s * GLA_CHUNK,
                       backward=False, fast=fast)
            _gla_chunk(qb_ref, kb_ref, vb_ref, bb_ref, ob_ref, sb_ref, ktmp_ref, btmp_ref,
                       (n_sub - 1 - s) * GLA_CHUNK, backward=True, fast=fast)

    pl.when(min_b >= GLA_SAFE_LOG_DECAY)(lambda: run(True))
    pl.when(min_b < GLA_SAFE_LOG_DECAY)(lambda: run(False))


def _gla(lay, gq, gk, gv, bdec, bmin):
    tb = lay.tb
    assert lay.tm % tb == 0
    tl, ts = lay.l // tb, lay.s // tb
    lat0 = lay.n_ctx // tb

    def fwd_blk(b, t):
        return jnp.where(t < tl, b * tl + t, lat0 + b * ts + (t - tl))

    def bwd_blk(b, t):
        return jnp.where(t < tl, b * tl + (tl - 1 - t), lat0 + b * ts + (ts - 1 - (t - tl)))

    def specs(blk, half):
        return [
            pl.BlockSpec((tb, GLA_K_W), lambda b, t: (blk(b, t), 0)),
            pl.BlockSpec((tb, GLA_K_W), lambda b, t: (blk(b, t), 0)),
            pl.BlockSpec((tb, GLA_V_W), lambda b, t: (blk(b, t), 0)),
            pl.BlockSpec((tb, GLA_K_W), lambda b, t: (blk(b, t), half)),
            pl.BlockSpec((SUBLANES, LANES), lambda b, t: (blk(b, t) // (lay.tm // tb), 0)),
        ]

    return pl.pallas_call(
        functools.partial(_gla_kernel, tb=tb),
        grid=(lay.b, tl + ts),
        in_specs=specs(fwd_blk, 0) + specs(bwd_blk, 1),
        out_specs=[
            pl.BlockSpec((tb, GLA_V_W), lambda b, t: (fwd_blk(b, t), 0)),
            pl.BlockSpec((tb, GLA_V_W), lambda b, t: (bwd_blk(b, t), 0)),
        ],
        out_shape=[jax.ShapeDtypeStruct((lay.n, GLA_V_W), F32)] * 2,
        scratch_shapes=[
            pltpu.VMEM((GLA_HEADS, GLA_DK, GLA_DV), F32),
            pltpu.VMEM((GLA_HEADS, GLA_DK, GLA_DV), F32),
            pltpu.VMEM((GLA_CHUNK, GLA_K_W), F32),
            pltpu.VMEM((GLA_CHUNK, GLA_K_W), F32),
        ],
        compiler_params=_params("arbitrary", "arbitrary"),
    )(gq, gk, gv, bdec, bmin, gq, gk, gv, bdec, bmin)


def _merge_kernel(x_ref, mod_ref, ya_ref, of_ref, ob_ref, sr_ref, sga_ref, sgg_ref, gnw_ref, wba_ref, wbg_ref, wo_ref,
                  o_ref):
    o = of_ref[...] + ob_ref[...]
    gnw = gnw_ref[...]
    parts = []
    for hd in range(GLA_HEADS):
        oh = o[:, hd * GLA_DV:(hd + 1) * GLA_DV]
        parts.append(oh * lax.rsqrt(jnp.mean(oh * oh, axis=-1, keepdims=True) + NORM_EPS) * gnw)
    y_gla = (jnp.concatenate(parts, axis=-1) * sr_ref[...].astype(F32)).astype(BF16)
    z = (sga_ref[...].astype(F32) * _dot(ya_ref[...], wba_ref[...])
         + sgg_ref[...].astype(F32) * _dot(y_gla, wbg_ref[...]))
    y = _dot(z.astype(BF16), wo_ref[...])
    o_ref[...] = x_ref[...] + _mod_slice(mod_ref, 5) * y


def _merge(lay, layer, x_all, mod, y_attn, o_f, o_b, sr, sga, sgg, gnw, wba, wbg, wo):
    tm = lay.tm
    row = pl.BlockSpec((tm, D_MODEL), lambda i: (i, 0))
    weight = _resident((None, D_MODEL, D_MODEL), lambda i: (layer, 0, 0))
    return pl.pallas_call(
        _merge_kernel,
        grid=(lay.tiles,),
        in_specs=[
            row,
            pl.BlockSpec((None, None, 1, N_MOD * D_MODEL), lambda i: (layer, lay.mod_row(i), 0, 0)),
            row, row, row, row, row, row,
            pl.BlockSpec((None, 1, GLA_DV), lambda i: (layer, 0, 0)),
            weight, weight, weight,
        ],
        out_specs=row,
        out_shape=jax.ShapeDtypeStruct((lay.n, D_MODEL), F32),
        compiler_params=_params("arbitrary"),
    )(x_all, mod, y_attn, o_f, o_b, sr, sga, sgg, gnw, wba, wbg, wo)


def kernel(x, c, ctx, c_ctx, w_mod, b_mod, norm_w, ffn1_w13, ffn1_w2, ffn2_w13, ffn2_w2, w_in, q_norm_w, k_norm_w,
           attn_sink, gla_gate_w_fwd, gla_gate_b_fwd, gla_gate_w_bwd, gla_gate_b_bwd, gla_norm_w, w_branch_attn,
           w_branch_gla, w_out):
    batch, seq, _ = x.shape
    ctx_len = ctx.shape[1]
    depth = w_mod.shape[0]
    assert batch + 1 <= MOD_ROWS
    lay = _Layout(batch, seq, ctx_len)
    lay_ffn = _Layout(batch, seq, ctx_len, FFN_TM_MAX)

    cond_rows = jnp.concatenate([c, c_ctx[None, :], jnp.zeros((MOD_ROWS - batch - 1, D_MODEL), F32)], axis=0)
    ctx_cols = 2 * ATTN_KV_W + GLA_K_W + GLA_V_W
    lr0, lr1 = ctx_cols, ctx_cols + 2 * GLA_RANK
    wm = jnp.concatenate([w_in[:, :, :lr0], w_in[:, :, lr1:]], axis=-1).astype(BF16)
    wlr = jnp.pad(w_in[:, :, lr0:lr1], ((0, 0), (0, 0), (0, LR_PAD - 2 * GLA_RANK))).astype(BF16)
    gw = jnp.zeros((depth, LR_PAD, 2 * GLA_K_W), F32)
    gw = gw.at[:, :GLA_RANK, :GLA_K_W].set(gla_gate_w_fwd).at[:, GLA_RANK:2 * GLA_RANK, GLA_K_W:].set(gla_gate_w_bwd)
    gw = gw.astype(BF16)
    gb = jnp.concatenate([gla_gate_b_fwd, gla_gate_b_bwd], axis=-1)[:, None, :]
    w13_1, w2_1 = ffn1_w13.astype(BF16), ffn1_w2.astype(BF16)
    w13_2, w2_2 = ffn2_w13.astype(BF16), ffn2_w2.astype(BF16)
    wba, wbg, wo = w_branch_attn.astype(BF16), w_branch_gla.astype(BF16), w_out.astype(BF16)
    qnw, knw, gnw = q_norm_w[:, None, :], k_norm_w[:, None, :], gla_norm_w[:, None, :]
    rope_c, rope_s = _rope_tables(lay)
    logit_bound = (HEAD_DIM * QK_SCALE_LOG2 * ATTN_BOUND_SLACK) * (
        jnp.max(jnp.abs(q_norm_w), axis=-1) * jnp.max(jnp.abs(k_norm_w), axis=-1))

    mod = _modulation(cond_rows, w_mod, b_mod).reshape(depth, MOD_ROWS, 1, N_MOD * D_MODEL)
    x_all = (ctx.reshape(batch * ctx_len, D_MODEL), x.reshape(batch * seq, D_MODEL))

    for layer in range(depth):
        last = layer == depth - 1
        x_all = _ffn(lay_ffn, layer, 0, x_all, mod, norm_w, w13_1, w2_1)
        q, kv, gq, gk, gv, bdec, sr, sga, sgg, bmin = _proj(lay, layer, x_all, mod, norm_w, wm, wlr, gw, gb, qnw, knw,
                                                       rope_c, rope_s)
        y_attn = _attn(lay, layer, attn_sink, logit_bound, q, kv)
        o_f, o_b = _gla(lay, gq, gk, gv, bdec, bmin)
        x_all = _merge(lay, layer, x_all, mod, y_attn, o_f, o_b, sr, sga, sgg, gnw, wba, wbg, wo)
        x_all = _ffn(lay_ffn, layer, 2, x_all, mod, norm_w, w13_2, w2_2, latents_only=last)
    return x_all.reshape(batch, seq, D_MODEL)
```

```python
import functools

import numpy as np
import jax
import jax.numpy as jnp
from jax import lax
from jax.experimental import pallas as pl
from jax.experimental.pallas import tpu as pltpu

F32 = jnp.float32
BF16 = jnp.bfloat16

D_MODEL = 1024
N_MOD = 9
NORM_EPS = 1e-6
D_FF = 2816
GRID_W = 64
ATTN_HEADS = 8
ATTN_KV_HEADS = 2
ATTN_GROUP = ATTN_HEADS // ATTN_KV_HEADS
HEAD_DIM = 128
ATTN_BLOCK = 128
ROPE_THETA = 10000.0
LOG2_E = 1.4426950408889634
QK_SCALE_LOG2 = HEAD_DIM ** -0.5 * LOG2_E
GLA_HEADS = 4
GLA_DK = 128
GLA_DV = 256
GLA_RANK = 16
GLA_TEMP = 16.0
GLA_CHUNK = 128
ATTN_Q_W = ATTN_HEADS * HEAD_DIM
ATTN_KV_W = ATTN_KV_HEADS * HEAD_DIM
GLA_K_W = GLA_HEADS * GLA_DK
GLA_V_W = GLA_HEADS * GLA_DV
GLA_SAFE_LOG_DECAY = -60.0 * LOG2_E

V7X_VMEM_BYTES = 64 * 1024 * 1024
VMEM_LIMIT = V7X_VMEM_BYTES - 8 * 1024 * 1024
MOD_ROWS = 16
MOD_COL_TILE = 1024
SUBLANES = 8
LANES = 128


def _dot(a, b):
    return jnp.dot(a, b, preferred_element_type=F32)


def _dot_nt(a, b):
    return lax.dot_general(a, b, (((1,), (1,)), ((), ())), preferred_element_type=F32)


def _dot_tn(a, b):
    return lax.dot_general(a, b, (((0,), (0,)), ((), ())), preferred_element_type=F32)


def _sigmoid(x):
    return 1.0 / (1.0 + jnp.exp(-x))


def _resident(block_shape, index_map):
    return pl.BlockSpec(block_shape, index_map, pipeline_mode=pl.Buffered(1))


def _params(*sem):
    return pltpu.CompilerParams(dimension_semantics=sem, vmem_limit_bytes=VMEM_LIMIT)


class _Layout:
    def __init__(self, batch, seq, ctx_len, tm_max=512):
        self.b, self.s, self.l = batch, seq, ctx_len
        self.n_ctx = batch * ctx_len
        self.n = self.n_ctx + batch * seq
        tm = tm_max
        while self.n_ctx % tm or seq % tm:
            tm //= 2
        assert tm >= GLA_CHUNK and tm % GLA_CHUNK == 0
        self.tm = tm
        self.ctx_tiles = self.n_ctx // tm
        self.tiles_per_batch = seq // tm
        self.tiles = self.n // tm
        tb = 256
        while ctx_len % tb or seq % tb:
            tb //= 2
        assert tb >= GLA_CHUNK
        self.tb = tb
        assert ctx_len % ATTN_BLOCK == 0 and seq % ATTN_BLOCK == 0 and seq % GRID_W == 0

    def mod_row(self, i):
        return jnp.where(i < self.ctx_tiles, self.b, (i - self.ctx_tiles) // self.tiles_per_batch)

    def rope_block(self, i):
        return jnp.where(i < self.ctx_tiles, 0, 1 + (i - self.ctx_tiles) % self.tiles_per_batch)


def _mod_kernel(c_ref, w_ref, b_ref, o_ref):
    c = c_ref[...]
    cond = (c * _sigmoid(c)).astype(BF16)
    o_ref[...] = _dot(cond, w_ref[...].astype(BF16)) + b_ref[...]


def _modulation(cond_rows, w_mod, b_mod):
    depth = w_mod.shape[0]
    n_cols = N_MOD * D_MODEL
    return pl.pallas_call(
        _mod_kernel,
        grid=(depth, n_cols // MOD_COL_TILE),
        in_specs=[
            pl.BlockSpec((MOD_ROWS, D_MODEL), lambda l, j: (0, 0)),
            pl.BlockSpec((None, D_MODEL, MOD_COL_TILE), lambda l, j: (l, 0, j)),
            pl.BlockSpec((None, 1, MOD_COL_TILE), lambda l, j: (l, 0, j)),
        ],
        out_specs=pl.BlockSpec((None, MOD_ROWS, MOD_COL_TILE), lambda l, j: (l, 0, j)),
        out_shape=jax.ShapeDtypeStruct((depth, MOD_ROWS, n_cols), F32),
        compiler_params=_params("arbitrary", "arbitrary"),
    )(cond_rows, w_mod, b_mod.reshape(depth, 1, n_cols))


def _mod_slice(mod_ref, k):
    return mod_ref[:, k * D_MODEL:(k + 1) * D_MODEL]


def _norm_modulate(x, nw, shift, scale):
    y = x * lax.rsqrt(jnp.mean(x * x, axis=-1, keepdims=True) + NORM_EPS) * nw
    return y * (1.0 + scale) + shift


FFN_CHUNKS = 11
FFN_TM_MAX = 1024


def _ffn_kernel(x_ref, mod_ref, nw_ref, w13_ref, w2_ref, o_ref, *, sub):
    _ffn_body(x_ref[...], mod_ref, nw_ref, w13_ref, w2_ref, o_ref, sub)


def _ffn_first_kernel(xc_ref, xl_ref, mod_ref, nw_ref, w13_ref, w2_ref, o_ref, *, sub, ctx_tiles):
    x = jnp.where(pl.program_id(0) < ctx_tiles, xc_ref[...], xl_ref[...])
    _ffn_body(x, mod_ref, nw_ref, w13_ref, w2_ref, o_ref, sub)


def _ffn_body(x, mod_ref, nw_ref, w13_ref, w2_ref, o_ref, sub):
    h = _norm_modulate(x, nw_ref[sub:sub + 1, :], _mod_slice(mod_ref, 3 * sub), _mod_slice(mod_ref, 3 * sub + 1))
    hb = h.astype(BF16)
    fc = D_FF // FFN_CHUNKS
    acc = None
    for c in range(FFN_CHUNKS):
        up = _dot(hb, w13_ref[:, c * fc:(c + 1) * fc])
        gate = _dot(hb, w13_ref[:, D_FF + c * fc:D_FF + (c + 1) * fc])
        a = (gate * _sigmoid(gate) * up).astype(BF16)
        part = _dot(a, w2_ref[c * fc:(c + 1) * fc, :])
        acc = part if acc is None else acc + part
    o_ref[...] = x + (0.5 * _mod_slice(mod_ref, 3 * sub + 2)) * acc


def _ffn(lay, layer, sub, x_rows, mod, norm_w, w13, w2, *, latents_only=False):
    tm = lay.tm
    off = lay.ctx_tiles if latents_only else 0
    n_tiles = lay.tiles - off
    if isinstance(x_rows, tuple):
        assert not latents_only
        body = functools.partial(_ffn_first_kernel, sub=sub, ctx_tiles=lay.ctx_tiles)
        row_specs = [
            pl.BlockSpec((tm, D_MODEL), lambda i: (jnp.minimum(i, lay.ctx_tiles - 1), 0)),
            pl.BlockSpec((tm, D_MODEL), lambda i: (jnp.maximum(i - lay.ctx_tiles, 0), 0)),
        ]
    else:
        body = functools.partial(_ffn_kernel, sub=sub)
        row_specs = [pl.BlockSpec((tm, D_MODEL), lambda i: (i + off, 0))]
        x_rows = (x_rows,)
    return pl.pallas_call(
        body,
        grid=(n_tiles,),
        in_specs=row_specs + [
            pl.BlockSpec((None, None, 1, N_MOD * D_MODEL), lambda i: (layer, lay.mod_row(i + off), 0, 0)),
            pl.BlockSpec((None, 3, D_MODEL), lambda i: (layer, 0, 0)),
            _resident((None, D_MODEL, 2 * D_FF), lambda i: (layer, 0, 0)),
            _resident((None, D_FF, D_MODEL), lambda i: (layer, 0, 0)),
        ],
        out_specs=pl.BlockSpec((tm, D_MODEL), lambda i: (i, 0)),
        out_shape=jax.ShapeDtypeStruct((n_tiles * tm, D_MODEL), F32),
        compiler_params=_params("arbitrary"),
    )(*x_rows, mod, norm_w, w13, w2)


C_AK, C_AV, C_GK, C_GV, C_AQ, C_GQ, C_GR, C_GA, C_GG, C_END = np.cumsum(
    [0, ATTN_KV_W, ATTN_KV_W, GLA_K_W, GLA_V_W, ATTN_Q_W, GLA_K_W, GLA_V_W, D_MODEL, D_MODEL]).tolist()
LR_PAD = 128
CUMSUM_ROWS = 256
PROJ_COL_CHUNK = 256


def _log_sigmoid(x):
    return jnp.minimum(x, 0.0) - jnp.log(1.0 + jnp.exp(-jnp.abs(x)))


def _proj_kernel(x_ref, mod_ref, nw_ref, wm_ref, wlr_ref, gw_ref, gb_ref, qnw_ref, knw_ref, rc_ref, rs_ref,
                 q_ref, kv_ref, gq_ref, gk_ref, gv_ref, bdec_ref, sr_ref, sga_ref, sgg_ref, bmin_ref):
    x = x_ref[...]
    tm = x.shape[0]
    h = _norm_modulate(x, nw_ref[1:2, :], _mod_slice(mod_ref, 3), _mod_slice(mod_ref, 4))
    hb = h.astype(BF16)
    rope_c = rc_ref[...]
    rope_s = rs_ref[...]
    lane = lax.broadcasted_iota(jnp.int32, (tm, HEAD_DIM), 1)
    first_half = (lane % (HEAD_DIM // 2)) < (HEAD_DIM // 4)

    def norm_rope(t, w):
        y = t * lax.rsqrt(jnp.mean(t * t, axis=-1, keepdims=True) + NORM_EPS) * w
        partner = jnp.where(first_half, pltpu.roll(y, HEAD_DIM - HEAD_DIM // 4, 1), pltpu.roll(y, HEAD_DIM // 4, 1))
        return y * rope_c + partner * rope_s

    def decay_gates():
        lr = _dot(hb, wlr_ref[...]).astype(BF16)
        g = _log_sigmoid(_dot(lr, gw_ref[...]) + gb_ref[...]) * (LOG2_E / GLA_TEMP)
        cr = min(CUMSUM_ROWS, tm)
        ri = lax.broadcasted_iota(jnp.int32, (cr, cr), 0)
        ci = lax.broadcasted_iota(jnp.int32, (cr, cr), 1)
        same = (ri // GLA_CHUNK) == (ci // GLA_CHUNK)
        t_fwd = jnp.where(same & (ci <= ri), 1.0, 0.0).astype(BF16)
        t_bwd = jnp.where(same & (ci >= ri), 1.0, 0.0).astype(BF16)
        b_min = None
        for blk in range(tm // cr):
            rows = slice(blk * cr, (blk + 1) * cr)
            for tri, cols in ((t_fwd, slice(0, GLA_K_W)), (t_bwd, slice(GLA_K_W, 2 * GLA_K_W))):
                gp = g[rows, cols]
                hi = gp.astype(BF16)
                lo = (gp - hi.astype(F32)).astype(BF16)
                b = _dot(tri, hi) + _dot(tri, lo)
                bdec_ref[rows, cols] = b
                b_min = b if b_min is None else jnp.minimum(b_min, b)
        b_min = jnp.min(jnp.min(b_min, axis=0, keepdims=True), axis=1, keepdims=True)
        bmin_ref[...] = jnp.broadcast_to(b_min, bmin_ref.shape)

    def project(c0, c1, out_ref, finish):
        for a in range(c0, c1, PROJ_COL_CHUNK):
            out_ref[:, a - c0:a - c0 + PROJ_COL_CHUNK] = finish(_dot(hb, wm_ref[:, a:a + PROJ_COL_CHUNK])).astype(BF16)

    decay_gates()
    kp = _dot(hb, wm_ref[:, C_AK:C_AV])
    knw = knw_ref[...]
    for i in range(ATTN_KV_HEADS):
        sl = slice(i * HEAD_DIM, (i + 1) * HEAD_DIM)
        kv_ref[:, sl] = norm_rope(kp[:, sl], knw).astype(BF16)
    kv_ref[:, ATTN_KV_W:] = _dot(hb, wm_ref[:, C_AV:C_GK]).astype(BF16)
    project(C_GK, C_GV, gk_ref, lambda t: t)
    project(C_GV, C_AQ, gv_ref, lambda t: t)
    qnw = qnw_ref[...]
    heads_per_chunk = PROJ_COL_CHUNK // HEAD_DIM
    for c in range(ATTN_HEADS // heads_per_chunk):
        qp = _dot(hb, wm_ref[:, C_AQ + c * PROJ_COL_CHUNK:C_AQ + (c + 1) * PROJ_COL_CHUNK])
        for ih in range(heads_per_chunk):
            i = c * heads_per_chunk + ih
            qh = (norm_rope(qp[:, ih * HEAD_DIM:(ih + 1) * HEAD_DIM], qnw) * QK_SCALE_LOG2).astype(BF16)
            for blk in range(tm // ATTN_BLOCK):
                dst = (blk * ATTN_HEADS + i) * ATTN_BLOCK
                q_ref[dst:dst + ATTN_BLOCK, :] = qh[blk * ATTN_BLOCK:(blk + 1) * ATTN_BLOCK, :]
    project(C_GQ, C_GR, gq_ref, lambda t: t * (GLA_DK ** -0.5))
    project(C_GR, C_GA, sr_ref, lambda t: t * _sigmoid(t))
    project(C_GA, C_GG, sga_ref, _sigmoid)
    project(C_GG, C_END, sgg_ref, _sigmoid)


def _proj(lay, layer, x_all, mod, norm_w, wm, wlr, gw, gb, qnw, knw, rope_c, rope_s):
    tm, n = lay.tm, lay.n
    row = lambda w: pl.BlockSpec((tm, w), lambda i: (i, 0))
    widths = [2 * ATTN_KV_W, GLA_K_W, GLA_K_W, GLA_V_W, 2 * GLA_K_W, GLA_V_W, D_MODEL, D_MODEL]
    dtypes = [BF16, BF16, BF16, BF16, F32, BF16, BF16, BF16]
    q_spec = pl.BlockSpec((tm * ATTN_HEADS, HEAD_DIM), lambda i: (i, 0))
    q_shape = jax.ShapeDtypeStruct((n * ATTN_HEADS, HEAD_DIM), BF16)
    return pl.pallas_call(
        _proj_kernel,
        grid=(lay.tiles,),
        in_specs=[
            row(D_MODEL),
            pl.BlockSpec((None, None, 1, N_MOD * D_MODEL), lambda i: (layer, lay.mod_row(i), 0, 0)),
            pl.BlockSpec((None, 3, D_MODEL), lambda i: (layer, 0, 0)),
            _resident((None, D_MODEL, C_END), lambda i: (layer, 0, 0)),
            _resident((None, D_MODEL, LR_PAD), lambda i: (layer, 0, 0)),
            _resident((None, LR_PAD, 2 * GLA_K_W), lambda i: (layer, 0, 0)),
            pl.BlockSpec((None, 1, 2 * GLA_K_W), lambda i: (layer, 0, 0)),
            pl.BlockSpec((None, 1, HEAD_DIM), lambda i: (layer, 0, 0)),
            pl.BlockSpec((None, 1, HEAD_DIM), lambda i: (layer, 0, 0)),
            pl.BlockSpec((tm, HEAD_DIM), lambda i: (lay.rope_block(i), 0)),
            pl.BlockSpec((tm, HEAD_DIM), lambda i: (lay.rope_block(i), 0)),
        ],
        out_specs=[q_spec] + [row(w) for w in widths] + [pl.BlockSpec((SUBLANES, LANES), lambda i: (i, 0))],
        out_shape=([q_shape] + [jax.ShapeDtypeStruct((n, w), dt) for w, dt in zip(widths, dtypes)]
                   + [jax.ShapeDtypeStruct((lay.tiles * SUBLANES, LANES), F32)]),
        compiler_params=_params("arbitrary"),
    )(x_all, mod, norm_w, wm, wlr, gw, gb, qnw, knw, rope_c, rope_s)


def _rope_tables(lay):
    pos = np.arange(lay.s)
    half = HEAD_DIM // 2
    inv_freq = ROPE_THETA ** (-np.arange(0, half, 2, dtype=np.float32) / half)
    inv_freq = jnp.asarray(inv_freq, F32)
    ang_r = jnp.asarray(pos // GRID_W, F32)[:, None] * inv_freq[None, :]
    ang_c = jnp.asarray(pos % GRID_W, F32)[:, None] * inv_freq[None, :]
    cos = jnp.concatenate([jnp.cos(ang_r)] * 2 + [jnp.cos(ang_c)] * 2, axis=-1)
    sin = jnp.concatenate([-jnp.sin(ang_r), jnp.sin(ang_r), -jnp.sin(ang_c), jnp.sin(ang_c)], axis=-1)
    cos = jnp.concatenate([jnp.ones((lay.tm, HEAD_DIM), F32), cos], axis=0)
    sin = jnp.concatenate([jnp.zeros((lay.tm, HEAD_DIM), F32), sin], axis=0)
    return cos, sin


NEG_BIG = -1e30
ATTN_FIXED_SHIFT_MAX = 50.0
ATTN_BLOCKS_PER_STEP = 2
ATTN_BOUND_SLACK = 1.02


def _attn_kernel(sink_ref, bound_ref, q_ref, kp_ref, kc_ref, kn_ref, kx_ref, o_ref, kcat_ref, vcat_ref, *,
                 layer, ctx_steps, lat_steps, ctx_len, nb):
    j = pl.program_id(1)
    jl = j - ctx_steps
    qb = ATTN_BLOCK
    rows = ATTN_GROUP * qb
    win = 3 * qb
    grp = lax.broadcasted_iota(jnp.int32, (rows, 1), 0) // qb

    def sink_column(kvh):
        sink = jnp.zeros((rows, 1), F32)
        for g in range(ATTN_GROUP):
            sink = jnp.where(grp == g, sink_ref[layer, kvh * ATTN_GROUP + g] * LOG2_E, sink)
        return sink

    bound = bound_ref[layer]
    small = bound <= ATTN_FIXED_SHIFT_MAX

    def q_rows(u, kvh):
        start = (u * ATTN_KV_HEADS + kvh) * rows
        return q_ref[start:start + rows, :]

    def finish(u, kvh, s, v, fixed_shift):
        sink = sink_column(kvh)
        m = jnp.maximum(bound, sink) if fixed_shift else jnp.maximum(jnp.max(s, axis=-1, keepdims=True), sink)
        p = jnp.exp2(s - m)
        den = jnp.sum(p, axis=-1, keepdims=True) + jnp.exp2(sink - m)
        o = _dot(p.astype(BF16), v) / den
        for g in range(ATTN_GROUP):
            hd = kvh * ATTN_GROUP + g
            o_ref[u * qb:(u + 1) * qb, hd * HEAD_DIM:(hd + 1) * HEAD_DIM] = o[g * qb:(g + 1) * qb].astype(o_ref.dtype)

    def context_queries(fixed_shift):
        for u in range(nb):
            for kvh in range(ATTN_KV_HEADS):
                k = kx_ref[:, kvh * HEAD_DIM:(kvh + 1) * HEAD_DIM]
                v = kx_ref[:, ATTN_KV_W + kvh * HEAD_DIM:ATTN_KV_W + (kvh + 1) * HEAD_DIM]
                finish(u, kvh, _dot_nt(q_rows(u, kvh), k), v, fixed_shift)

    def window_piece(p):
        if p < 0:
            return kp_ref, (nb + p) * qb, jl > 0
        if p >= nb:
            return kn_ref, (p - nb) * qb, jl < lat_steps - 1
        return kc_ref, p * qb, None

    def latent_queries(fixed_shift):
        row_i = lax.broadcasted_iota(jnp.int32, (rows, qb), 0) % qb
        col_i = lax.broadcasted_iota(jnp.int32, (rows, qb), 1)
        for u in range(nb):
            pieces = [window_piece(u + d) for d in (-1, 0, 1)]
            see_prev = col_i >= row_i if pieces[0][2] is None else (col_i >= row_i) & pieces[0][2]
            see_next = col_i <= row_i if pieces[2][2] is None else (col_i <= row_i) & pieces[2][2]
            for kvh in range(ATTN_KV_HEADS):
                ksl = slice(kvh * HEAD_DIM, (kvh + 1) * HEAD_DIM)
                vsl = slice(ATTN_KV_W + kvh * HEAD_DIM, ATTN_KV_W + (kvh + 1) * HEAD_DIM)
                for idx, (ref, r0, _) in enumerate(pieces):
                    kcat_ref[idx * qb:(idx + 1) * qb, :] = ref[r0:r0 + qb, ksl]
                    vcat_ref[idx * qb:(idx + 1) * qb, :] = ref[r0:r0 + qb, vsl]
                kcat_ref[win:win + ctx_len, :] = kx_ref[:, ksl]
                vcat_ref[win:win + ctx_len, :] = kx_ref[:, vsl]
                s = _dot_nt(q_rows(u, kvh), kcat_ref[...])
                s = jnp.concatenate([jnp.where(see_prev, s[:, :qb], NEG_BIG), s[:, qb:2 * qb],
                                     jnp.where(see_next, s[:, 2 * qb:win], NEG_BIG), s[:, win:]], axis=1)
                finish(u, kvh, s, vcat_ref[...], fixed_shift)

    is_ctx = j < ctx_steps
    pl.when(is_ctx & small)(lambda: context_queries(True))
    pl.when(is_ctx & jnp.logical_not(small))(lambda: context_queries(False))
    pl.when(jnp.logical_not(is_ctx) & small)(lambda: latent_queries(True))
    pl.when(jnp.logical_not(is_ctx) & jnp.logical_not(small))(lambda: latent_queries(False))


def _attn(lay, layer, sink, bound, q, kv):
    nb = ATTN_BLOCKS_PER_STEP
    while lay.l % (nb * ATTN_BLOCK) or lay.s % (nb * ATTN_BLOCK):
        nb //= 2
    qb = ATTN_BLOCK
    span = nb * qb
    cb, lb = lay.l // span, lay.s // span
    lat0 = lay.n_ctx // span

    def q_blk(b, j):
        return jnp.where(j < cb, b * cb + j, lat0 + b * lb + (j - cb))

    def win_blk(delta):
        def f(b, j):
            return (lat0 + b * lb + jnp.clip(j - cb + delta, 0, lb - 1), 0)
        return f

    return pl.pallas_call(
        functools.partial(_attn_kernel, layer=layer, ctx_steps=cb, lat_steps=lb, ctx_len=lay.l, nb=nb),
        grid=(lay.b, cb + lb),
        in_specs=[
            pl.BlockSpec(memory_space=pltpu.SMEM),
            pl.BlockSpec(memory_space=pltpu.SMEM),
            pl.BlockSpec((span * ATTN_HEADS, HEAD_DIM), lambda b, j: (q_blk(b, j), 0)),
            pl.BlockSpec((span, 2 * ATTN_KV_W), win_blk(-1)),
            pl.BlockSpec((span, 2 * ATTN_KV_W), win_blk(0)),
            pl.BlockSpec((span, 2 * ATTN_KV_W), win_blk(1)),
            pl.BlockSpec((lay.l, 2 * ATTN_KV_W), lambda b, j: (b, 0)),
        ],
        out_specs=pl.BlockSpec((span, ATTN_Q_W), lambda b, j: (q_blk(b, j), 0)),
        out_shape=jax.ShapeDtypeStruct((lay.n, ATTN_Q_W), BF16),
        scratch_shapes=[
            pltpu.VMEM((3 * qb + lay.l, HEAD_DIM), BF16),
            pltpu.VMEM((3 * qb + lay.l, HEAD_DIM), BF16),
        ],
        compiler_params=_params("arbitrary", "arbitrary"),
    )(sink, bound, q, kv, kv, kv, kv)


def _gla_chunk(q_ref, k_ref, v_ref, b_ref, o_ref, s_ref, kf_ref, bf_ref, r0, *, backward, fast):
    ch = GLA_CHUNK
    rows = slice(r0, r0 + ch)
    b = b_ref[rows, :]
    q = q_ref[rows, :]
    k = k_ref[rows, :]
    v = v_ref[rows, :]
    end = 0 if backward else ch - 1
    b_end = b[end:end + 1, :]
    qd = q * jnp.exp2(b).astype(BF16)
    k_end = k * jnp.exp2(b_end - b).astype(BF16)
    decay_end = jnp.broadcast_to(jnp.exp2(b_end), (GLA_DK, GLA_K_W))
    ri = lax.broadcasted_iota(jnp.int32, (ch, ch), 0)
    ci = lax.broadcasted_iota(jnp.int32, (ch, ch), 1)
    visible = (ci >= ri) if backward else (ci <= ri)
    diag = lax.broadcasted_iota(jnp.int32, (GLA_DK, GLA_DK), 0) == lax.broadcasted_iota(jnp.int32, (GLA_DK, GLA_DK), 1)
    if fast:
        kd = k * jnp.exp2(-b).astype(BF16)
    else:
        kf_ref[...] = k.astype(F32)
        bf_ref[...] = b
        q32 = q.astype(F32)
    for hd in range(GLA_HEADS):
        ksl = slice(hd * GLA_DK, (hd + 1) * GLA_DK)
        vsl = slice(hd * GLA_DV, (hd + 1) * GLA_DV)
        if fast:
            a = _dot_nt(qd[:, ksl], kd[:, ksl])
        else:
            q_h, b_h = q32[:, ksl], b[:, ksl]

            def columns(grp, a_acc, q_h=q_h, b_h=b_h, ksl=ksl):
                base = pl.multiple_of(grp * SUBLANES, SUBLANES)
                k_rows = kf_ref[pl.ds(base, SUBLANES), ksl]
                b_rows = bf_ref[pl.ds(base, SUBLANES), ksl]
                for r in range(SUBLANES):
                    decay = jnp.exp2(jnp.minimum(b_h - b_rows[r:r + 1, :], 0.0))
                    col = jnp.sum(q_h * k_rows[r:r + 1, :] * decay, axis=-1, keepdims=True)
                    a_acc = jnp.where(ci == base + r, col, a_acc)
                return a_acc

            a = lax.fori_loop(0, ch // SUBLANES, columns, jnp.zeros((ch, ch), F32))
        a = jnp.where(visible, a, 0.0).astype(BF16)
        both = _dot(jnp.concatenate([a, k_end[:, ksl].T], axis=0), v[:, vsl])
        st = s_ref[hd]
        o_ref[rows, vsl] = (_dot(qd[:, ksl], st.astype(BF16)) + both[:ch]).astype(o_ref.dtype)
        decay_col = jnp.sum(jnp.where(diag, decay_end[:, ksl], 0.0), axis=-1, keepdims=True)
        s_ref[hd] = st * decay_col + both[ch:]


def _gla_kernel(qf_ref, kf_ref, vf_ref, bf_ref, mf_ref, qb_ref, kb_ref, vb_ref, bb_ref, mb_ref, of_ref, ob_ref,
                sf_ref, sb_ref, ktmp_ref, btmp_ref, *, tb):
    @pl.when(pl.program_id(1) == 0)
    def _():
        sf_ref[...] = jnp.zeros_like(sf_ref)
        sb_ref[...] = jnp.zeros_like(sb_ref)

    n_sub = tb // GLA_CHUNK
    both = jnp.minimum(mf_ref[...], mb_ref[...])
    min_b = jnp.min(jnp.min(both, axis=0, keepdims=True), axis=1, keepdims=True)[0, 0]

    def run(fast):
        for s in range(n_sub):
            _gla_chunk(qf_ref, kf_ref, vf_ref, bf_ref, of_ref, sf_ref, ktmp_ref, btmp_ref, s * GLA_CHUNK,
                       backward=False, fast=fast)
            _gla_chunk(qb_ref, kb_ref, vb_ref, bb_ref, ob_ref, sb_ref, ktmp_ref, btmp_ref,
                       (n_sub - 1 - s) * GLA_CHUNK, backward=True, fast=fast)

    pl.when(min_b >= GLA_SAFE_LOG_DECAY)(lambda: run(True))
    pl.when(min_b < GLA_SAFE_LOG_DECAY)(lambda: run(False))


def _gla(lay, gq, gk, gv, bdec, bmin):
    tb = lay.tb
    assert lay.tm % tb == 0
    tl, ts = lay.l // tb, lay.s // tb
    lat0 = lay.n_ctx // tb

    def fwd_blk(b, t):
        return jnp.where(t < tl, b * tl + t, lat0 + b * ts + (t - tl))

    def bwd_blk(b, t):
        return jnp.where(t < tl, b * tl + (tl - 1 - t), lat0 + b * ts + (ts - 1 - (t - tl)))

    def specs(blk, half):
        return [
            pl.BlockSpec((tb, GLA_K_W), lambda b, t: (blk(b, t), 0)),
            pl.BlockSpec((tb, GLA_K_W), lambda b, t: (blk(b, t), 0)),
            pl.BlockSpec((tb, GLA_V_W), lambda b, t: (blk(b, t), 0)),
            pl.BlockSpec((tb, GLA_K_W), lambda b, t: (blk(b, t), half)),
            pl.BlockSpec((SUBLANES, LANES), lambda b, t: (blk(b, t) // (lay.tm // tb), 0)),
        ]

    return pl.pallas_call(
        functools.partial(_gla_kernel, tb=tb),
        grid=(lay.b, tl + ts),
        in_specs=specs(fwd_blk, 0) + specs(bwd_blk, 1),
        out_specs=[
            pl.BlockSpec((tb, GLA_V_W), lambda b, t: (fwd_blk(b, t), 0)),
            pl.BlockSpec((tb, GLA_V_W), lambda b, t: (bwd_blk(b, t), 0)),
        ],
        out_shape=[jax.ShapeDtypeStruct((lay.n, GLA_V_W), BF16)] * 2,
        scratch_shapes=[
            pltpu.VMEM((GLA_HEADS, GLA_DK, GLA_DV), F32),
            pltpu.VMEM((GLA_HEADS, GLA_DK, GLA_DV), F32),
            pltpu.VMEM((GLA_CHUNK, GLA_K_W), F32),
            pltpu.VMEM((GLA_CHUNK, GLA_K_W), F32),
        ],
        compiler_params=_params("arbitrary", "arbitrary"),
    )(gq, gk, gv, bdec, bmin, gq, gk, gv, bdec, bmin)


def _merge_kernel(x_ref, mod_ref, ya_ref, of_ref, ob_ref, sr_ref, sga_ref, sgg_ref, gnw_ref, wba_ref, wbg_ref, wo_ref,
                  o_ref):
    o = of_ref[...].astype(F32) + ob_ref[...].astype(F32)
    gnw = gnw_ref[...]
    parts = []
    for hd in range(GLA_HEADS):
        oh = o[:, hd * GLA_DV:(hd + 1) * GLA_DV]
        parts.append(oh * lax.rsqrt(jnp.mean(oh * oh, axis=-1, keepdims=True) + NORM_EPS) * gnw)
    y_gla = (jnp.concatenate(parts, axis=-1) * sr_ref[...].astype(F32)).astype(BF16)
    z = (sga_ref[...].astype(F32) * _dot(ya_ref[...], wba_ref[...])
         + sgg_ref[...].astype(F32) * _dot(y_gla, wbg_ref[...]))
    y = _dot(z.astype(BF16), wo_ref[...])
    o_ref[...] = x_ref[...] + _mod_slice(mod_ref, 5) * y


def _merge(lay, layer, x_all, mod, y_attn, o_f, o_b, sr, sga, sgg, gnw, wba, wbg, wo):
    tm = lay.tm
    row = pl.BlockSpec((tm, D_MODEL), lambda i: (i, 0))
    weight = _resident((None, D_MODEL, D_MODEL), lambda i: (layer, 0, 0))
    return pl.pallas_call(
        _merge_kernel,
        grid=(lay.tiles,),
        in_specs=[
            row,
            pl.BlockSpec((None, None, 1, N_MOD * D_MODEL), lambda i: (layer, lay.mod_row(i), 0, 0)),
            row, row, row, row, row, row,
            pl.BlockSpec((None, 1, GLA_DV), lambda i: (layer, 0, 0)),
            weight, weight, weight,
        ],
        out_specs=row,
        out_shape=jax.ShapeDtypeStruct((lay.n, D_MODEL), F32),
        compiler_params=_params("arbitrary"),
    )(x_all, mod, y_attn, o_f, o_b, sr, sga, sgg, gnw, wba, wbg, wo)


def kernel(x, c, ctx, c_ctx, w_mod, b_mod, norm_w, ffn1_w13, ffn1_w2, ffn2_w13, ffn2_w2, w_in, q_norm_w, k_norm_w,
           attn_sink, gla_gate_w_fwd, gla_gate_b_fwd, gla_gate_w_bwd, gla_gate_b_bwd, gla_norm_w, w_branch_attn,
           w_branch_gla, w_out):
    batch, seq, _ = x.shape
    ctx_len = ctx.shape[1]
    depth = w_mod.shape[0]
    assert batch + 1 <= MOD_ROWS
    lay = _Layout(batch, seq, ctx_len)
    lay_ffn = _Layout(batch, seq, ctx_len, FFN_TM_MAX)

    cond_rows = jnp.concatenate([c, c_ctx[None, :], jnp.zeros((MOD_ROWS - batch - 1, D_MODEL), F32)], axis=0)
    ctx_cols = 2 * ATTN_KV_W + GLA_K_W + GLA_V_W
    lr0, lr1 = ctx_cols, ctx_cols + 2 * GLA_RANK
    wm = jnp.concatenate([w_in[:, :, :lr0], w_in[:, :, lr1:]], axis=-1).astype(BF16)
    wlr = jnp.pad(w_in[:, :, lr0:lr1], ((0, 0), (0, 0), (0, LR_PAD - 2 * GLA_RANK))).astype(BF16)
    gw = jnp.zeros((depth, LR_PAD, 2 * GLA_K_W), F32)
    gw = gw.at[:, :GLA_RANK, :GLA_K_W].set(gla_gate_w_fwd).at[:, GLA_RANK:2 * GLA_RANK, GLA_K_W:].set(gla_gate_w_bwd)
    gw = gw.astype(BF16)
    gb = jnp.concatenate([gla_gate_b_fwd, gla_gate_b_bwd], axis=-1)[:, None, :]
    w13_1, w2_1 = ffn1_w13.astype(BF16), ffn1_w2.astype(BF16)
    w13_2, w2_2 = ffn2_w13.astype(BF16), ffn2_w2.astype(BF16)
    wba, wbg, wo = w_branch_attn.astype(BF16), w_branch_gla.astype(BF16), w_out.astype(BF16)
    qnw, knw, gnw = q_norm_w[:, None, :], k_norm_w[:, None, :], gla_norm_w[:, None, :]
    rope_c, rope_s = _rope_tables(lay)
    logit_bound = (HEAD_DIM * QK_SCALE_LOG2 * ATTN_BOUND_SLACK) * (
        jnp.max(jnp.abs(q_norm_w), axis=-1) * jnp.max(jnp.abs(k_norm_w), axis=-1))

    mod = _modulation(cond_rows, w_mod, b_mod).reshape(depth, MOD_ROWS, 1, N_MOD * D_MODEL)
    x_all = (ctx.reshape(batch * ctx_len, D_MODEL), x.reshape(batch * seq, D_MODEL))

    for layer in range(depth):
        last = layer == depth - 1
        x_all = _ffn(lay_ffn, layer, 0, x_all, mod, norm_w, w13_1, w2_1)
        q, kv, gq, gk, gv, bdec, sr, sga, sgg, bmin = _proj(lay, layer, x_all, mod, norm_w, wm, wlr, gw, gb, qnw, knw,
                                                       rope_c, rope_s)
        y_attn = _attn(lay, layer, attn_sink, logit_bound, q, kv)
        o_f, o_b = _gla(lay, gq, gk, gv, bdec, bmin)
        x_all = _merge(lay, layer, x_all, mod, y_attn, o_f, o_b, sr, sga, sgg, gnw, wba, wbg, wo)
        x_all = _ffn(lay_ffn, layer, 2, x_all, mod, norm_w, w13_2, w2_2, latents_only=last)
    return x_all.reshape(batch, seq, D_MODEL)
```

```python
import functools
from typing import NamedTuple

import numpy as np
import jax
import jax.numpy as jnp
from jax import lax
from jax.experimental import pallas as pl
from jax.experimental.pallas import tpu as pltpu

F32 = jnp.float32
BF16 = jnp.bfloat16

D_MODEL = 1024
N_MOD = 9
NORM_EPS = 1e-6
D_FF = 2816
GRID_W = 64
ATTN_HEADS = 8
ATTN_KV_HEADS = 2
ATTN_GROUP = ATTN_HEADS // ATTN_KV_HEADS
HEAD_DIM = 128
ATTN_BLOCK = 128
ROPE_THETA = 10000.0
LOG2_E = 1.4426950408889634
QK_SCALE_LOG2 = HEAD_DIM ** -0.5 * LOG2_E
GLA_HEADS = 4
GLA_DK = 128
GLA_DV = 256
GLA_RANK = 16
GLA_TEMP = 16.0
GLA_CHUNK = 128
ATTN_Q_W = ATTN_HEADS * HEAD_DIM
ATTN_KV_W = ATTN_KV_HEADS * HEAD_DIM
GLA_K_W = GLA_HEADS * GLA_DK
GLA_V_W = GLA_HEADS * GLA_DV
GLA_SAFE_LOG_DECAY = -60.0 * LOG2_E

V7X_VMEM_BYTES = 64 * 1024 * 1024
VMEM_LIMIT = V7X_VMEM_BYTES - 8 * 1024 * 1024
MOD_ROWS = 16
MOD_COL_TILE = 1024
SUBLANES = 8
LANES = 128


def _dot(a, b):
    return jnp.dot(a, b, preferred_element_type=F32)


def _dot_nt(a, b):
    return lax.dot_general(a, b, (((1,), (1,)), ((), ())), preferred_element_type=F32)


def _dot_tn(a, b):
    return lax.dot_general(a, b, (((0,), (0,)), ((), ())), preferred_element_type=F32)


def _sigmoid(x):
    return 1.0 / (1.0 + jnp.exp(-x))


def _resident(block_shape, index_map):
    return pl.BlockSpec(block_shape, index_map, pipeline_mode=pl.Buffered(1))


def _params(*sem):
    return pltpu.CompilerParams(dimension_semantics=sem, vmem_limit_bytes=VMEM_LIMIT)


class _Layout:
    def __init__(self, batch, seq, ctx_len, tm_max=512):
        self.b, self.s, self.l = batch, seq, ctx_len
        self.n_ctx = batch * ctx_len
        self.n = self.n_ctx + batch * seq
        tm = tm_max
        while self.n_ctx % tm or seq % tm:
            tm //= 2
        assert tm >= GLA_CHUNK and tm % GLA_CHUNK == 0
        self.tm = tm
        self.ctx_tiles = self.n_ctx // tm
        self.tiles_per_batch = seq // tm
        self.tiles = self.n // tm
        tb = 256
        while ctx_len % tb or seq % tb:
            tb //= 2
        assert tb >= GLA_CHUNK
        self.tb = tb
        assert ctx_len % ATTN_BLOCK == 0 and seq % ATTN_BLOCK == 0 and seq % GRID_W == 0

    def mod_row(self, i):
        return jnp.where(i < self.ctx_tiles, self.b, (i - self.ctx_tiles) // self.tiles_per_batch)

    def rope_block(self, i):
        return jnp.where(i < self.ctx_tiles, 0, 1 + (i - self.ctx_tiles) % self.tiles_per_batch)


def _mod_kernel(c_ref, w_ref, b_ref, o_ref):
    c = c_ref[...]
    cond = (c * _sigmoid(c)).astype(BF16)
    o_ref[...] = _dot(cond, w_ref[...].astype(BF16)) + b_ref[...]


def _modulation(cond_rows, w_mod, b_mod):
    depth = w_mod.shape[0]
    n_cols = N_MOD * D_MODEL
    return pl.pallas_call(
        _mod_kernel,
        grid=(depth, n_cols // MOD_COL_TILE),
        in_specs=[
            pl.BlockSpec((MOD_ROWS, D_MODEL), lambda l, j: (0, 0)),
            pl.BlockSpec((None, D_MODEL, MOD_COL_TILE), lambda l, j: (l, 0, j)),
            pl.BlockSpec((None, 1, MOD_COL_TILE), lambda l, j: (l, 0, j)),
        ],
        out_specs=pl.BlockSpec((None, MOD_ROWS, MOD_COL_TILE), lambda l, j: (l, 0, j)),
        out_shape=jax.ShapeDtypeStruct((depth, MOD_ROWS, n_cols), F32),
        compiler_params=_params("arbitrary", "arbitrary"),
    )(cond_rows, w_mod, b_mod.reshape(depth, 1, n_cols))


def _mod_slice(mod_ref, k):
    return mod_ref[:, k * D_MODEL:(k + 1) * D_MODEL]


def _norm_modulate(x, nw, shift, scale):
    y = x * lax.rsqrt(jnp.mean(x * x, axis=-1, keepdims=True) + NORM_EPS) * nw
    return y * (1.0 + scale) + shift


FFN_CHUNKS = 11
FFN_TM_MAX = 1024


def _ffn_kernel(x_ref, mod_ref, nw_ref, w13_ref, w2_ref, o_ref, *, sub):
    _ffn_body(x_ref[...], mod_ref, nw_ref, w13_ref, w2_ref, o_ref, sub)


def _ffn_first_kernel(xc_ref, xl_ref, mod_ref, nw_ref, w13_ref, w2_ref, o_ref, *, sub, ctx_tiles):
    x = jnp.where(pl.program_id(0) < ctx_tiles, xc_ref[...], xl_ref[...])
    _ffn_body(x, mod_ref, nw_ref, w13_ref, w2_ref, o_ref, sub)


def _ffn_body(x, mod_ref, nw_ref, w13_ref, w2_ref, o_ref, sub):
    h = _norm_modulate(x, nw_ref[sub:sub + 1, :], _mod_slice(mod_ref, 3 * sub), _mod_slice(mod_ref, 3 * sub + 1))
    hb = h.astype(BF16)
    fc = D_FF // FFN_CHUNKS
    acc = None
    for c in range(FFN_CHUNKS):
        up = _dot(hb, w13_ref[:, c * fc:(c + 1) * fc])
        gate = _dot(hb, w13_ref[:, D_FF + c * fc:D_FF + (c + 1) * fc])
        a = (gate * _sigmoid(gate) * up).astype(BF16)
        part = _dot(a, w2_ref[c * fc:(c + 1) * fc, :])
        acc = part if acc is None else acc + part
    o_ref[...] = x + (0.5 * _mod_slice(mod_ref, 3 * sub + 2)) * acc


def _ffn(lay, layer, sub, x_rows, mod, norm_w, w13, w2, *, latents_only=False):
    tm = lay.tm
    off = lay.ctx_tiles if latents_only else 0
    n_tiles = lay.tiles - off
    if isinstance(x_rows, tuple):
        assert not latents_only
        body = functools.partial(_ffn_first_kernel, sub=sub, ctx_tiles=lay.ctx_tiles)
        row_specs = [
            pl.BlockSpec((tm, D_MODEL), lambda i: (jnp.minimum(i, lay.ctx_tiles - 1), 0)),
            pl.BlockSpec((tm, D_MODEL), lambda i: (jnp.maximum(i - lay.ctx_tiles, 0), 0)),
        ]
    else:
        body = functools.partial(_ffn_kernel, sub=sub)
        row_specs = [pl.BlockSpec((tm, D_MODEL), lambda i: (i + off, 0))]
        x_rows = (x_rows,)
    return pl.pallas_call(
        body,
        grid=(n_tiles,),
        in_specs=row_specs + [
            pl.BlockSpec((None, None, 1, N_MOD * D_MODEL), lambda i: (layer, lay.mod_row(i + off), 0, 0)),
            pl.BlockSpec((None, 3, D_MODEL), lambda i: (layer, 0, 0)),
            _resident((None, D_MODEL, 2 * D_FF), lambda i: (layer, 0, 0)),
            _resident((None, D_FF, D_MODEL), lambda i: (layer, 0, 0)),
        ],
        out_specs=pl.BlockSpec((tm, D_MODEL), lambda i: (i, 0)),
        out_shape=jax.ShapeDtypeStruct((n_tiles * tm, D_MODEL), F32),
        compiler_params=_params("arbitrary"),
    )(*x_rows, mod, norm_w, w13, w2)


C_AK, C_AV, C_GK, C_GV, C_AQ, C_GQ, C_GR, C_GA, C_GG, C_END = np.cumsum(
    [0, ATTN_KV_W, ATTN_KV_W, GLA_K_W, GLA_V_W, ATTN_Q_W, GLA_K_W, GLA_V_W, D_MODEL, D_MODEL]).tolist()
LR_PAD = 128
CUMSUM_ROWS = 256
PROJ_COL_CHUNK = 256


def _log_sigmoid(x):
    return jnp.minimum(x, 0.0) - jnp.log(1.0 + jnp.exp(-jnp.abs(x)))


def _proj_kernel(x_ref, mod_ref, nw_ref, wm_ref, wlr_ref, gw_ref, gb_ref, qnw_ref, knw_ref, rc_ref, rs_ref,
                 q_ref, kv_ref, gq_ref, gk_ref, gv_ref, bdec_ref, sr_ref, sga_ref, sgg_ref, bmin_ref):
    x = x_ref[...]
    tm = x.shape[0]
    h = _norm_modulate(x, nw_ref[1:2, :], _mod_slice(mod_ref, 3), _mod_slice(mod_ref, 4))
    hb = h.astype(BF16)
    rope_c = rc_ref[...]
    rope_s = rs_ref[...]
    lane = lax.broadcasted_iota(jnp.int32, (tm, HEAD_DIM), 1)
    first_half = (lane % (HEAD_DIM // 2)) < (HEAD_DIM // 4)

    def norm_rope(t, w):
        y = t * lax.rsqrt(jnp.mean(t * t, axis=-1, keepdims=True) + NORM_EPS) * w
        partner = jnp.where(first_half, pltpu.roll(y, HEAD_DIM - HEAD_DIM // 4, 1), pltpu.roll(y, HEAD_DIM // 4, 1))
        return y * rope_c + partner * rope_s

    def decay_gates():
        lr = _dot(hb, wlr_ref[...]).astype(BF16)
        g = _log_sigmoid(_dot(lr, gw_ref[...]) + gb_ref[...]) * (LOG2_E / GLA_TEMP)
        cr = min(CUMSUM_ROWS, tm)
        ri = lax.broadcasted_iota(jnp.int32, (cr, cr), 0)
        ci = lax.broadcasted_iota(jnp.int32, (cr, cr), 1)
        same = (ri // GLA_CHUNK) == (ci // GLA_CHUNK)
        t_fwd = jnp.where(same & (ci <= ri), 1.0, 0.0).astype(BF16)
        t_bwd = jnp.where(same & (ci >= ri), 1.0, 0.0).astype(BF16)
        b_min = None
        for blk in range(tm // cr):
            rows = slice(blk * cr, (blk + 1) * cr)
            for tri, cols in ((t_fwd, slice(0, GLA_K_W)), (t_bwd, slice(GLA_K_W, 2 * GLA_K_W))):
                gp = g[rows, cols]
                hi = gp.astype(BF16)
                lo = (gp - hi.astype(F32)).astype(BF16)
                b = _dot(tri, hi) + _dot(tri, lo)
                bdec_ref[rows, cols] = b
                b_min = b if b_min is None else jnp.minimum(b_min, b)
        b_min = jnp.min(jnp.min(b_min, axis=0, keepdims=True), axis=1, keepdims=True)
        bmin_ref[...] = jnp.broadcast_to(b_min, bmin_ref.shape)

    def project(c0, c1, out_ref, finish):
        for a in range(c0, c1, PROJ_COL_CHUNK):
            out_ref[:, a - c0:a - c0 + PROJ_COL_CHUNK] = finish(_dot(hb, wm_ref[:, a:a + PROJ_COL_CHUNK])).astype(BF16)

    decay_gates()
    kp = _dot(hb, wm_ref[:, C_AK:C_AV])
    knw = knw_ref[...]
    for i in range(ATTN_KV_HEADS):
        sl = slice(i * HEAD_DIM, (i + 1) * HEAD_DIM)
        kv_ref[:, sl] = norm_rope(kp[:, sl], knw).astype(BF16)
    kv_ref[:, ATTN_KV_W:] = _dot(hb, wm_ref[:, C_AV:C_GK]).astype(BF16)
    project(C_GK, C_GV, gk_ref, lambda t: t)
    project(C_GV, C_AQ, gv_ref, lambda t: t)
    qnw = qnw_ref[...]
    heads_per_chunk = PROJ_COL_CHUNK // HEAD_DIM
    for c in range(ATTN_HEADS // heads_per_chunk):
        qp = _dot(hb, wm_ref[:, C_AQ + c * PROJ_COL_CHUNK:C_AQ + (c + 1) * PROJ_COL_CHUNK])
        for ih in range(heads_per_chunk):
            i = c * heads_per_chunk + ih
            qh = (norm_rope(qp[:, ih * HEAD_DIM:(ih + 1) * HEAD_DIM], qnw) * QK_SCALE_LOG2).astype(BF16)
            for blk in range(tm // ATTN_BLOCK):
                dst = (blk * ATTN_HEADS + i) * ATTN_BLOCK
                q_ref[dst:dst + ATTN_BLOCK, :] = qh[blk * ATTN_BLOCK:(blk + 1) * ATTN_BLOCK, :]
    project(C_GQ, C_GR, gq_ref, lambda t: t * (GLA_DK ** -0.5))
    project(C_GR, C_GA, sr_ref, lambda t: t * _sigmoid(t))
    project(C_GA, C_GG, sga_ref, _sigmoid)
    project(C_GG, C_END, sgg_ref, _sigmoid)


def _proj(lay, layer, x_all, mod, norm_w, wm, wlr, gw, gb, qnw, knw, rope_c, rope_s):
    tm, n = lay.tm, lay.n
    row = lambda w: pl.BlockSpec((tm, w), lambda i: (i, 0))
    widths = [2 * ATTN_KV_W, GLA_K_W, GLA_K_W, GLA_V_W, 2 * GLA_K_W, GLA_V_W, D_MODEL, D_MODEL]
    dtypes = [BF16, BF16, BF16, BF16, F32, BF16, BF16, BF16]
    q_spec = pl.BlockSpec((tm * ATTN_HEADS, HEAD_DIM), lambda i: (i, 0))
    q_shape = jax.ShapeDtypeStruct((n * ATTN_HEADS, HEAD_DIM), BF16)
    return pl.pallas_call(
        _proj_kernel,
        grid=(lay.tiles,),
        in_specs=[
            row(D_MODEL),
            pl.BlockSpec((None, None, 1, N_MOD * D_MODEL), lambda i: (layer, lay.mod_row(i), 0, 0)),
            pl.BlockSpec((None, 3, D_MODEL), lambda i: (layer, 0, 0)),
            _resident((None, D_MODEL, C_END), lambda i: (layer, 0, 0)),
            _resident((None, D_MODEL, LR_PAD), lambda i: (layer, 0, 0)),
            _resident((None, LR_PAD, 2 * GLA_K_W), lambda i: (layer, 0, 0)),
            pl.BlockSpec((None, 1, 2 * GLA_K_W), lambda i: (layer, 0, 0)),
            pl.BlockSpec((None, 1, HEAD_DIM), lambda i: (layer, 0, 0)),
            pl.BlockSpec((None, 1, HEAD_DIM), lambda i: (layer, 0, 0)),
            pl.BlockSpec((tm, HEAD_DIM), lambda i: (lay.rope_block(i), 0)),
            pl.BlockSpec((tm, HEAD_DIM), lambda i: (lay.rope_block(i), 0)),
        ],
        out_specs=[q_spec] + [row(w) for w in widths] + [pl.BlockSpec((SUBLANES, LANES), lambda i: (i, 0))],
        out_shape=([q_shape] + [jax.ShapeDtypeStruct((n, w), dt) for w, dt in zip(widths, dtypes)]
                   + [jax.ShapeDtypeStruct((lay.tiles * SUBLANES, LANES), F32)]),
        compiler_params=_params("arbitrary"),
    )(x_all, mod, norm_w, wm, wlr, gw, gb, qnw, knw, rope_c, rope_s)


def _rope_tables(lay):
    pos = np.arange(lay.s)
    half = HEAD_DIM // 2
    inv_freq = ROPE_THETA ** (-np.arange(0, half, 2, dtype=np.float32) / half)
    inv_freq = jnp.asarray(inv_freq, F32)
    ang_r = jnp.asarray(pos // GRID_W, F32)[:, None] * inv_freq[None, :]
    ang_c = jnp.asarray(pos % GRID_W, F32)[:, None] * inv_freq[None, :]
    cos = jnp.concatenate([jnp.cos(ang_r)] * 2 + [jnp.cos(ang_c)] * 2, axis=-1)
    sin = jnp.concatenate([-jnp.sin(ang_r), jnp.sin(ang_r), -jnp.sin(ang_c), jnp.sin(ang_c)], axis=-1)
    cos = jnp.concatenate([jnp.ones((lay.tm, HEAD_DIM), F32), cos], axis=0)
    sin = jnp.concatenate([jnp.zeros((lay.tm, HEAD_DIM), F32), sin], axis=0)
    return cos, sin


NEG_BIG = -1e30
ATTN_FIXED_SHIFT_MAX = 50.0
ATTN_BLOCKS_PER_STEP = 2
ATTN_BOUND_SLACK = 1.02


def _attn_kernel(sink_ref, bound_ref, q_ref, kp_ref, kc_ref, kn_ref, kx_ref, o_ref, kcat_ref, vcat_ref, *,
                 layer, ctx_steps, lat_steps, ctx_len, nb):
    j = pl.program_id(1)
    jl = j - ctx_steps
    qb = ATTN_BLOCK
    rows = ATTN_GROUP * qb
    win = 3 * qb
    grp = lax.broadcasted_iota(jnp.int32, (rows, 1), 0) // qb

    def sink_column(kvh):
        sink = jnp.zeros((rows, 1), F32)
        for g in range(ATTN_GROUP):
            sink = jnp.where(grp == g, sink_ref[layer, kvh * ATTN_GROUP + g] * LOG2_E, sink)
        return sink

    bound = bound_ref[layer]
    small = bound <= ATTN_FIXED_SHIFT_MAX

    def q_rows(u, kvh):
        start = (u * ATTN_KV_HEADS + kvh) * rows
        return q_ref[start:start + rows, :]

    def finish(u, kvh, s, v, fixed_shift):
        sink = sink_column(kvh)
        m = jnp.maximum(bound, sink) if fixed_shift else jnp.maximum(jnp.max(s, axis=-1, keepdims=True), sink)
        p = jnp.exp2(s - m)
        den = jnp.sum(p, axis=-1, keepdims=True) + jnp.exp2(sink - m)
        o = _dot(p.astype(BF16), v) / den
        for g in range(ATTN_GROUP):
            hd = kvh * ATTN_GROUP + g
            o_ref[u * qb:(u + 1) * qb, hd * HEAD_DIM:(hd + 1) * HEAD_DIM] = o[g * qb:(g + 1) * qb].astype(o_ref.dtype)

    def context_queries(fixed_shift):
        for u in range(nb):
            for kvh in range(ATTN_KV_HEADS):
                k = kx_ref[:, kvh * HEAD_DIM:(kvh + 1) * HEAD_DIM]
                v = kx_ref[:, ATTN_KV_W + kvh * HEAD_DIM:ATTN_KV_W + (kvh + 1) * HEAD_DIM]
                finish(u, kvh, _dot_nt(q_rows(u, kvh), k), v, fixed_shift)

    def window_piece(p):
        if p < 0:
            return kp_ref, (nb + p) * qb, jl > 0
        if p >= nb:
            return kn_ref, (p - nb) * qb, jl < lat_steps - 1
        return kc_ref, p * qb, None

    def latent_queries(fixed_shift):
        row_i = lax.broadcasted_iota(jnp.int32, (rows, qb), 0) % qb
        col_i = lax.broadcasted_iota(jnp.int32, (rows, qb), 1)
        for u in range(nb):
            pieces = [window_piece(u + d) for d in (-1, 0, 1)]
            see_prev = col_i >= row_i if pieces[0][2] is None else (col_i >= row_i) & pieces[0][2]
            see_next = col_i <= row_i if pieces[2][2] is None else (col_i <= row_i) & pieces[2][2]
            for kvh in range(ATTN_KV_HEADS):
                ksl = slice(kvh * HEAD_DIM, (kvh + 1) * HEAD_DIM)
                vsl = slice(ATTN_KV_W + kvh * HEAD_DIM, ATTN_KV_W + (kvh + 1) * HEAD_DIM)
                for idx, (ref, r0, _) in enumerate(pieces):
                    kcat_ref[idx * qb:(idx + 1) * qb, :] = ref[r0:r0 + qb, ksl]
                    vcat_ref[idx * qb:(idx + 1) * qb, :] = ref[r0:r0 + qb, vsl]
                kcat_ref[win:win + ctx_len, :] = kx_ref[:, ksl]
                vcat_ref[win:win + ctx_len, :] = kx_ref[:, vsl]
                s = _dot_nt(q_rows(u, kvh), kcat_ref[...])
                s = jnp.concatenate([jnp.where(see_prev, s[:, :qb], NEG_BIG), s[:, qb:2 * qb],
                                     jnp.where(see_next, s[:, 2 * qb:win], NEG_BIG), s[:, win:]], axis=1)
                finish(u, kvh, s, vcat_ref[...], fixed_shift)

    is_ctx = j < ctx_steps
    pl.when(is_ctx & small)(lambda: context_queries(True))
    pl.when(is_ctx & jnp.logical_not(small))(lambda: context_queries(False))
    pl.when(jnp.logical_not(is_ctx) & small)(lambda: latent_queries(True))
    pl.when(jnp.logical_not(is_ctx) & jnp.logical_not(small))(lambda: latent_queries(False))


def _attn(lay, layer, sink, bound, q, kv):
    nb = ATTN_BLOCKS_PER_STEP
    while lay.l % (nb * ATTN_BLOCK) or lay.s % (nb * ATTN_BLOCK):
        nb //= 2
    qb = ATTN_BLOCK
    span = nb * qb
    cb, lb = lay.l // span, lay.s // span
    lat0 = lay.n_ctx // span

    def q_blk(b, j):
        return jnp.where(j < cb, b * cb + j, lat0 + b * lb + (j - cb))

    def win_blk(delta):
        def f(b, j):
            return (lat0 + b * lb + jnp.clip(j - cb + delta, 0, lb - 1), 0)
        return f

    return pl.pallas_call(
        functools.partial(_attn_kernel, layer=layer, ctx_steps=cb, lat_steps=lb, ctx_len=lay.l, nb=nb),
        grid=(lay.b, cb + lb),
        in_specs=[
            pl.BlockSpec(memory_space=pltpu.SMEM),
            pl.BlockSpec(memory_space=pltpu.SMEM),
            pl.BlockSpec((span * ATTN_HEADS, HEAD_DIM), lambda b, j: (q_blk(b, j), 0)),
            pl.BlockSpec((span, 2 * ATTN_KV_W), win_blk(-1)),
            pl.BlockSpec((span, 2 * ATTN_KV_W), win_blk(0)),
            pl.BlockSpec((span, 2 * ATTN_KV_W), win_blk(1)),
            pl.BlockSpec((lay.l, 2 * ATTN_KV_W), lambda b, j: (b, 0)),
        ],
        out_specs=pl.BlockSpec((span, ATTN_Q_W), lambda b, j: (q_blk(b, j), 0)),
        out_shape=jax.ShapeDtypeStruct((lay.n, ATTN_Q_W), BF16),
        scratch_shapes=[
            pltpu.VMEM((3 * qb + lay.l, HEAD_DIM), BF16),
            pltpu.VMEM((3 * qb + lay.l, HEAD_DIM), BF16),
        ],
        compiler_params=_params("arbitrary", "arbitrary"),
    )(sink, bound, q, kv, kv, kv, kv)


class _GlaJob(NamedTuple):
    q_ref: object
    k_ref: object
    v_ref: object
    b_ref: object
    o_ref: object
    s_ref: object
    row0: int
    backward: bool


class _GlaOperands(NamedTuple):
    rows: slice
    b: jax.Array
    q: jax.Array
    k: jax.Array
    v: jax.Array
    qd: jax.Array
    kd: object
    k_end: jax.Array
    decay_end: jax.Array
    visible: jax.Array


def _gla_chunks(jobs, kf_ref, bf_ref, *, fast):
    ch = GLA_CHUNK
    ri = lax.broadcasted_iota(jnp.int32, (ch, ch), 0)
    ci = lax.broadcasted_iota(jnp.int32, (ch, ch), 1)
    diag = lax.broadcasted_iota(jnp.int32, (GLA_DK, GLA_DK), 0) == lax.broadcasted_iota(jnp.int32, (GLA_DK, GLA_DK), 1)
    ksls = [slice(hd * GLA_DK, (hd + 1) * GLA_DK) for hd in range(GLA_HEADS)]
    vsls = [slice(hd * GLA_DV, (hd + 1) * GLA_DV) for hd in range(GLA_HEADS)]
    heads = range(GLA_HEADS)

    def operands(job):
        rows = slice(job.row0, job.row0 + ch)
        b = job.b_ref[rows, :]
        q = job.q_ref[rows, :]
        k = job.k_ref[rows, :]
        end = 0 if job.backward else ch - 1
        b_end = b[end:end + 1, :]
        return _GlaOperands(
            rows=rows, b=b, q=q, k=k, v=job.v_ref[rows, :],
            qd=q * jnp.exp2(b).astype(BF16),
            kd=k * jnp.exp2(-b).astype(BF16) if fast else None,
            k_end=k * jnp.exp2(b_end - b).astype(BF16),
            decay_end=jnp.broadcast_to(jnp.exp2(b_end), (GLA_DK, GLA_K_W)),
            visible=(ci >= ri) if job.backward else (ci <= ri))

    def exact_scores(x, ksl):
        kf_ref[...] = x.k.astype(F32)
        bf_ref[...] = x.b
        q_h, b_h = x.q.astype(F32)[:, ksl], x.b[:, ksl]

        def columns(grp, a_acc):
            base = pl.multiple_of(grp * SUBLANES, SUBLANES)
            k_rows = kf_ref[pl.ds(base, SUBLANES), ksl]
            b_rows = bf_ref[pl.ds(base, SUBLANES), ksl]
            for r in range(SUBLANES):
                decay = jnp.exp2(jnp.minimum(b_h - b_rows[r:r + 1, :], 0.0))
                col = jnp.sum(q_h * k_rows[r:r + 1, :] * decay, axis=-1, keepdims=True)
                a_acc = jnp.where(ci == base + r, col, a_acc)
            return a_acc

        return lax.fori_loop(0, ch // SUBLANES, columns, jnp.zeros((ch, ch), F32))

    ops = [operands(job) for job in jobs]
    if fast:
        scores = [[_dot_nt(x.qd[:, ksls[hd]], x.kd[:, ksls[hd]]) for hd in heads] for x in ops]
    else:
        scores = [[exact_scores(x, ksls[hd]) for hd in heads] for x in ops]
    inter = [[_dot(x.qd[:, ksls[hd]], job.s_ref[hd].astype(BF16)) for hd in heads] for job, x in zip(jobs, ops)]
    lhs = [[jnp.concatenate([jnp.where(x.visible, a, 0.0).astype(BF16), x.k_end[:, ksls[hd]].T], axis=0)
            for hd, a in zip(heads, row)] for x, row in zip(ops, scores)]
    both = [[_dot(lhs_h, x.v[:, vsls[hd]]) for hd, lhs_h in zip(heads, row)] for x, row in zip(ops, lhs)]
    for job, x, inter_j, both_j in zip(jobs, ops, inter, both):
        for hd in heads:
            job.o_ref[x.rows, vsls[hd]] = (inter_j[hd] + both_j[hd][:ch]).astype(job.o_ref.dtype)
            decay_col = jnp.sum(jnp.where(diag, x.decay_end[:, ksls[hd]], 0.0), axis=-1, keepdims=True)
            job.s_ref[hd] = job.s_ref[hd] * decay_col + both_j[hd][ch:]


def _gla_kernel(qf_ref, kf_ref, vf_ref, bf_ref, mf_ref, qb_ref, kb_ref, vb_ref, bb_ref, mb_ref, of_ref, ob_ref,
                sf_ref, sb_ref, ktmp_ref, btmp_ref, *, tb):
    @pl.when(pl.program_id(1) == 0)
    def _():
        sf_ref[...] = jnp.zeros_like(sf_ref)
        sb_ref[...] = jnp.zeros_like(sb_ref)

    n_sub = tb // GLA_CHUNK
    both = jnp.minimum(mf_ref[...], mb_ref[...])
    min_b = jnp.min(jnp.min(both, axis=0, keepdims=True), axis=1, keepdims=True)[0, 0]

    def run(fast):
        for s in range(n_sub):
            jobs = [_GlaJob(qf_ref, kf_ref, vf_ref, bf_ref, of_ref, sf_ref, s * GLA_CHUNK, False),
                    _GlaJob(qb_ref, kb_ref, vb_ref, bb_ref, ob_ref, sb_ref, (n_sub - 1 - s) * GLA_CHUNK, True)]
            _gla_chunks(jobs, ktmp_ref, btmp_ref, fast=fast)

    pl.when(min_b >= GLA_SAFE_LOG_DECAY)(lambda: run(True))
    pl.when(min_b < GLA_SAFE_LOG_DECAY)(lambda: run(False))


def _gla(lay, gq, gk, gv, bdec, bmin):
    tb = lay.tb
    assert lay.tm % tb == 0
    tl, ts = lay.l // tb, lay.s // tb
    lat0 = lay.n_ctx // tb

    def fwd_blk(b, t):
        return jnp.where(t < tl, b * tl + t, lat0 + b * ts + (t - tl))

    def bwd_blk(b, t):
        return jnp.where(t < tl, b * tl + (tl - 1 - t), lat0 + b * ts + (ts - 1 - (t - tl)))

    def specs(blk, half):
        return [
            pl.BlockSpec((tb, GLA_K_W), lambda b, t: (blk(b, t), 0)),
            pl.BlockSpec((tb, GLA_K_W), lambda b, t: (blk(b, t), 0)),
            pl.BlockSpec((tb, GLA_V_W), lambda b, t: (blk(b, t), 0)),
            pl.BlockSpec((tb, GLA_K_W), lambda b, t: (blk(b, t), half)),
            pl.BlockSpec((SUBLANES, LANES), lambda b, t: (blk(b, t) // (lay.tm // tb), 0)),
        ]

    return pl.pallas_call(
        functools.partial(_gla_kernel, tb=tb),
        grid=(lay.b, tl + ts),
        in_specs=specs(fwd_blk, 0) + specs(bwd_blk, 1),
        out_specs=[
            pl.BlockSpec((tb, GLA_V_W), lambda b, t: (fwd_blk(b, t), 0)),
            pl.BlockSpec((tb, GLA_V_W), lambda b, t: (bwd_blk(b, t), 0)),
        ],
        out_shape=[jax.ShapeDtypeStruct((lay.n, GLA_V_W), BF16)] * 2,
        scratch_shapes=[
            pltpu.VMEM((GLA_HEADS, GLA_DK, GLA_DV), F32),
            pltpu.VMEM((GLA_HEADS, GLA_DK, GLA_DV), F32),
            pltpu.VMEM((GLA_CHUNK, GLA_K_W), F32),
            pltpu.VMEM((GLA_CHUNK, GLA_K_W), F32),
        ],
        compiler_params=_params("arbitrary", "arbitrary"),
    )(gq, gk, gv, bdec, bmin, gq, gk, gv, bdec, bmin)


def _merge_kernel(x_ref, mod_ref, ya_ref, of_ref, ob_ref, sr_ref, sga_ref, sgg_ref, gnw_ref, wba_ref, wbg_ref, wo_ref,
                  o_ref):
    o = of_ref[...].astype(F32) + ob_ref[...].astype(F32)
    gnw = gnw_ref[...]
    parts = []
    for hd in range(GLA_HEADS):
        oh = o[:, hd * GLA_DV:(hd + 1) * GLA_DV]
        parts.append(oh * lax.rsqrt(jnp.mean(oh * oh, axis=-1, keepdims=True) + NORM_EPS) * gnw)
    y_gla = (jnp.concatenate(parts, axis=-1) * sr_ref[...].astype(F32)).astype(BF16)
    z = (sga_ref[...].astype(F32) * _dot(ya_ref[...], wba_ref[...])
         + sgg_ref[...].astype(F32) * _dot(y_gla, wbg_ref[...]))
    y = _dot(z.astype(BF16), wo_ref[...])
    o_ref[...] = x_ref[...] + _mod_slice(mod_ref, 5) * y


def _merge(lay, layer, x_all, mod, y_attn, o_f, o_b, sr, sga, sgg, gnw, wba, wbg, wo):
    tm = lay.tm
    row = pl.BlockSpec((tm, D_MODEL), lambda i: (i, 0))
    weight = _resident((None, D_MODEL, D_MODEL), lambda i: (layer, 0, 0))
    return pl.pallas_call(
        _merge_kernel,
        grid=(lay.tiles,),
        in_specs=[
            row,
            pl.BlockSpec((None, None, 1, N_MOD * D_MODEL), lambda i: (layer, lay.mod_row(i), 0, 0)),
            row, row, row, row, row, row,
            pl.BlockSpec((None, 1, GLA_DV), lambda i: (layer, 0, 0)),
            weight, weight, weight,
        ],
        out_specs=row,
        out_shape=jax.ShapeDtypeStruct((lay.n, D_MODEL), F32),
        compiler_params=_params("arbitrary"),
    )(x_all, mod, y_attn, o_f, o_b, sr, sga, sgg, gnw, wba, wbg, wo)


def kernel(x, c, ctx, c_ctx, w_mod, b_mod, norm_w, ffn1_w13, ffn1_w2, ffn2_w13, ffn2_w2, w_in, q_norm_w, k_norm_w,
           attn_sink, gla_gate_w_fwd, gla_gate_b_fwd, gla_gate_w_bwd, gla_gate_b_bwd, gla_norm_w, w_branch_attn,
           w_branch_gla, w_out):
    batch, seq, _ = x.shape
    ctx_len = ctx.shape[1]
    depth = w_mod.shape[0]
    assert batch + 1 <= MOD_ROWS
    lay = _Layout(batch, seq, ctx_len)
    lay_ffn = _Layout(batch, seq, ctx_len, FFN_TM_MAX)

    cond_rows = jnp.concatenate([c, c_ctx[None, :], jnp.zeros((MOD_ROWS - batch - 1, D_MODEL), F32)], axis=0)
    ctx_cols = 2 * ATTN_KV_W + GLA_K_W + GLA_V_W
    lr0, lr1 = ctx_cols, ctx_cols + 2 * GLA_RANK
    wm = jnp.concatenate([w_in[:, :, :lr0], w_in[:, :, lr1:]], axis=-1).astype(BF16)
    wlr = jnp.pad(w_in[:, :, lr0:lr1], ((0, 0), (0, 0), (0, LR_PAD - 2 * GLA_RANK))).astype(BF16)
    gw = jnp.zeros((depth, LR_PAD, 2 * GLA_K_W), F32)
    gw = gw.at[:, :GLA_RANK, :GLA_K_W].set(gla_gate_w_fwd).at[:, GLA_RANK:2 * GLA_RANK, GLA_K_W:].set(gla_gate_w_bwd)
    gw = gw.astype(BF16)
    gb = jnp.concatenate([gla_gate_b_fwd, gla_gate_b_bwd], axis=-1)[:, None, :]
    w13_1, w2_1 = ffn1_w13.astype(BF16), ffn1_w2.astype(BF16)
    w13_2, w2_2 = ffn2_w13.astype(BF16), ffn2_w2.astype(BF16)
    wba, wbg, wo = w_branch_attn.astype(BF16), w_branch_gla.astype(BF16), w_out.astype(BF16)
    qnw, knw, gnw = q_norm_w[:, None, :], k_norm_w[:, None, :], gla_norm_w[:, None, :]
    rope_c, rope_s = _rope_tables(lay)
    logit_bound = (HEAD_DIM * QK_SCALE_LOG2 * ATTN_BOUND_SLACK) * (
        jnp.max(jnp.abs(q_norm_w), axis=-1) * jnp.max(jnp.abs(k_norm_w), axis=-1))

    mod = _modulation(cond_rows, w_mod, b_mod).reshape(depth, MOD_ROWS, 1, N_MOD * D_MODEL)
    x_all = (ctx.reshape(batch * ctx_len, D_MODEL), x.reshape(batch * seq, D_MODEL))

    for layer in range(depth):
        last = layer == depth - 1
        x_all = _ffn(lay_ffn, layer, 0, x_all, mod, norm_w, w13_1, w2_1)
        q, kv, gq, gk, gv, bdec, sr, sga, sgg, bmin = _proj(lay, layer, x_all, mod, norm_w, wm, wlr, gw, gb, qnw, knw,
                                                       rope_c, rope_s)
        y_attn = _attn(lay, layer, attn_sink, logit_bound, q, kv)
        o_f, o_b = _gla(lay, gq, gk, gv, bdec, bmin)
        x_all = _merge(lay, layer, x_all, mod, y_attn, o_f, o_b, sr, sga, sgg, gnw, wba, wbg, wo)
        x_all = _ffn(lay_ffn, layer, 2, x_all, mod, norm_w, w13_2, w2_2, latents_only=last)
    return x_all.reshape(batch, seq, D_MODEL)
```

```python
import functools
from typing import NamedTuple

import numpy as np
import jax
import jax.numpy as jnp
from jax import lax
from jax.experimental import pallas as pl
from jax.experimental.pallas import tpu as pltpu

F32 = jnp.float32
BF16 = jnp.bfloat16

D_MODEL = 1024
N_MOD = 9
NORM_EPS = 1e-6
D_FF = 2816
GRID_W = 64
ATTN_HEADS = 8
ATTN_KV_HEADS = 2
ATTN_GROUP = ATTN_HEADS // ATTN_KV_HEADS
HEAD_DIM = 128
ATTN_BLOCK = 128
ROPE_THETA = 10000.0
LOG2_E = 1.4426950408889634
QK_SCALE_LOG2 = HEAD_DIM ** -0.5 * LOG2_E
GLA_HEADS = 4
GLA_DK = 128
GLA_DV = 256
GLA_RANK = 16
GLA_TEMP = 16.0
GLA_CHUNK = 256
ATTN_Q_W = ATTN_HEADS * HEAD_DIM
ATTN_KV_W = ATTN_KV_HEADS * HEAD_DIM
GLA_K_W = GLA_HEADS * GLA_DK
GLA_V_W = GLA_HEADS * GLA_DV
GLA_SAFE_LOG_DECAY = -60.0 * LOG2_E

V7X_VMEM_BYTES = 64 * 1024 * 1024
VMEM_LIMIT = V7X_VMEM_BYTES - 8 * 1024 * 1024
MOD_ROWS = 16
MOD_COL_TILE = 1024
SUBLANES = 8
LANES = 128


def _dot(a, b):
    return jnp.dot(a, b, preferred_element_type=F32)


def _dot_nt(a, b):
    return lax.dot_general(a, b, (((1,), (1,)), ((), ())), preferred_element_type=F32)


def _dot_tn(a, b):
    return lax.dot_general(a, b, (((0,), (0,)), ((), ())), preferred_element_type=F32)


def _sigmoid(x):
    return 1.0 / (1.0 + jnp.exp(-x))


def _resident(block_shape, index_map):
    return pl.BlockSpec(block_shape, index_map, pipeline_mode=pl.Buffered(1))


def _params(*sem):
    return pltpu.CompilerParams(dimension_semantics=sem, vmem_limit_bytes=VMEM_LIMIT)


class _Layout:
    def __init__(self, batch, seq, ctx_len, tm_max=512):
        self.b, self.s, self.l = batch, seq, ctx_len
        self.n_ctx = batch * ctx_len
        self.n = self.n_ctx + batch * seq
        tm = tm_max
        while self.n_ctx % tm or seq % tm:
            tm //= 2
        assert tm >= GLA_CHUNK and tm % GLA_CHUNK == 0
        self.tm = tm
        self.ctx_tiles = self.n_ctx // tm
        self.tiles_per_batch = seq // tm
        self.tiles = self.n // tm
        tb = 256
        while ctx_len % tb or seq % tb:
            tb //= 2
        assert tb >= GLA_CHUNK
        self.tb = tb
        assert ctx_len % ATTN_BLOCK == 0 and seq % ATTN_BLOCK == 0 and seq % GRID_W == 0

    def mod_row(self, i):
        return jnp.where(i < self.ctx_tiles, self.b, (i - self.ctx_tiles) // self.tiles_per_batch)

    def rope_block(self, i):
        return jnp.where(i < self.ctx_tiles, 0, 1 + (i - self.ctx_tiles) % self.tiles_per_batch)


def _mod_kernel(c_ref, w_ref, b_ref, o_ref):
    c = c_ref[...]
    cond = (c * _sigmoid(c)).astype(BF16)
    o_ref[...] = _dot(cond, w_ref[...].astype(BF16)) + b_ref[...]


def _modulation(cond_rows, w_mod, b_mod):
    depth = w_mod.shape[0]
    n_cols = N_MOD * D_MODEL
    return pl.pallas_call(
        _mod_kernel,
        grid=(depth, n_cols // MOD_COL_TILE),
        in_specs=[
            pl.BlockSpec((MOD_ROWS, D_MODEL), lambda l, j: (0, 0)),
            pl.BlockSpec((None, D_MODEL, MOD_COL_TILE), lambda l, j: (l, 0, j)),
            pl.BlockSpec((None, 1, MOD_COL_TILE), lambda l, j: (l, 0, j)),
        ],
        out_specs=pl.BlockSpec((None, MOD_ROWS, MOD_COL_TILE), lambda l, j: (l, 0, j)),
        out_shape=jax.ShapeDtypeStruct((depth, MOD_ROWS, n_cols), F32),
        compiler_params=_params("arbitrary", "arbitrary"),
    )(cond_rows, w_mod, b_mod.reshape(depth, 1, n_cols))


def _mod_slice(mod_ref, k):
    return mod_ref[:, k * D_MODEL:(k + 1) * D_MODEL]


def _norm_modulate(x, nw, shift, scale):
    y = x * lax.rsqrt(jnp.mean(x * x, axis=-1, keepdims=True) + NORM_EPS) * nw
    return y * (1.0 + scale) + shift


FFN_CHUNKS = 11
FFN_TM_MAX = 1024


def _ffn_kernel(x_ref, mod_ref, nw_ref, w13_ref, w2_ref, o_ref, *, sub):
    _ffn_body(x_ref[...], mod_ref, nw_ref, w13_ref, w2_ref, o_ref, sub)


def _ffn_first_kernel(xc_ref, xl_ref, mod_ref, nw_ref, w13_ref, w2_ref, o_ref, *, sub, ctx_tiles):
    x = jnp.where(pl.program_id(0) < ctx_tiles, xc_ref[...], xl_ref[...])
    _ffn_body(x, mod_ref, nw_ref, w13_ref, w2_ref, o_ref, sub)


def _ffn_body(x, mod_ref, nw_ref, w13_ref, w2_ref, o_ref, sub):
    h = _norm_modulate(x, nw_ref[sub:sub + 1, :], _mod_slice(mod_ref, 3 * sub), _mod_slice(mod_ref, 3 * sub + 1))
    hb = h.astype(BF16)
    fc = D_FF // FFN_CHUNKS
    acc = None
    for c in range(FFN_CHUNKS):
        up = _dot(hb, w13_ref[:, c * fc:(c + 1) * fc])
        gate = _dot(hb, w13_ref[:, D_FF + c * fc:D_FF + (c + 1) * fc])
        a = (gate * _sigmoid(gate) * up).astype(BF16)
        part = _dot(a, w2_ref[c * fc:(c + 1) * fc, :])
        acc = part if acc is None else acc + part
    o_ref[...] = x + (0.5 * _mod_slice(mod_ref, 3 * sub + 2)) * acc


def _ffn(lay, layer, sub, x_rows, mod, norm_w, w13, w2, *, latents_only=False):
    tm = lay.tm
    off = lay.ctx_tiles if latents_only else 0
    n_tiles = lay.tiles - off
    if isinstance(x_rows, tuple):
        assert not latents_only
        body = functools.partial(_ffn_first_kernel, sub=sub, ctx_tiles=lay.ctx_tiles)
        row_specs = [
            pl.BlockSpec((tm, D_MODEL), lambda i: (jnp.minimum(i, lay.ctx_tiles - 1), 0)),
            pl.BlockSpec((tm, D_MODEL), lambda i: (jnp.maximum(i - lay.ctx_tiles, 0), 0)),
        ]
    else:
        body = functools.partial(_ffn_kernel, sub=sub)
        row_specs = [pl.BlockSpec((tm, D_MODEL), lambda i: (i + off, 0))]
        x_rows = (x_rows,)
    return pl.pallas_call(
        body,
        grid=(n_tiles,),
        in_specs=row_specs + [
            pl.BlockSpec((None, None, 1, N_MOD * D_MODEL), lambda i: (layer, lay.mod_row(i + off), 0, 0)),
            pl.BlockSpec((None, 3, D_MODEL), lambda i: (layer, 0, 0)),
            _resident((None, D_MODEL, 2 * D_FF), lambda i: (layer, 0, 0)),
            _resident((None, D_FF, D_MODEL), lambda i: (layer, 0, 0)),
        ],
        out_specs=pl.BlockSpec((tm, D_MODEL), lambda i: (i, 0)),
        out_shape=jax.ShapeDtypeStruct((n_tiles * tm, D_MODEL), F32),
        compiler_params=_params("arbitrary"),
    )(*x_rows, mod, norm_w, w13, w2)


C_AK, C_AV, C_GK, C_GV, C_AQ, C_GQ, C_GR, C_GA, C_GG, C_END = np.cumsum(
    [0, ATTN_KV_W, ATTN_KV_W, GLA_K_W, GLA_V_W, ATTN_Q_W, GLA_K_W, GLA_V_W, D_MODEL, D_MODEL]).tolist()
LR_PAD = 128
CUMSUM_ROWS = 256
PROJ_COL_CHUNK = 256


def _log_sigmoid(x):
    return jnp.minimum(x, 0.0) - jnp.log(1.0 + jnp.exp(-jnp.abs(x)))


def _proj_kernel(x_ref, mod_ref, nw_ref, wm_ref, wlr_ref, gw_ref, gb_ref, qnw_ref, knw_ref, rc_ref, rs_ref,
                 q_ref, kv_ref, gq_ref, gk_ref, gv_ref, bdec_ref, sr_ref, sga_ref, sgg_ref, bmin_ref):
    x = x_ref[...]
    tm = x.shape[0]
    h = _norm_modulate(x, nw_ref[1:2, :], _mod_slice(mod_ref, 3), _mod_slice(mod_ref, 4))
    hb = h.astype(BF16)
    rope_c = rc_ref[...]
    rope_s = rs_ref[...]
    lane = lax.broadcasted_iota(jnp.int32, (tm, HEAD_DIM), 1)
    first_half = (lane % (HEAD_DIM // 2)) < (HEAD_DIM // 4)

    def norm_rope(t, w):
        y = t * lax.rsqrt(jnp.mean(t * t, axis=-1, keepdims=True) + NORM_EPS) * w
        partner = jnp.where(first_half, pltpu.roll(y, HEAD_DIM - HEAD_DIM // 4, 1), pltpu.roll(y, HEAD_DIM // 4, 1))
        return y * rope_c + partner * rope_s

    def decay_gates():
        lr = _dot(hb, wlr_ref[...]).astype(BF16)
        g = _log_sigmoid(_dot(lr, gw_ref[...]) + gb_ref[...]) * (LOG2_E / GLA_TEMP)
        cr = min(CUMSUM_ROWS, tm)
        ri = lax.broadcasted_iota(jnp.int32, (cr, cr), 0)
        ci = lax.broadcasted_iota(jnp.int32, (cr, cr), 1)
        same = (ri // GLA_CHUNK) == (ci // GLA_CHUNK)
        t_fwd = jnp.where(same & (ci <= ri), 1.0, 0.0).astype(BF16)
        t_bwd = jnp.where(same & (ci >= ri), 1.0, 0.0).astype(BF16)
        b_min = None
        for blk in range(tm // cr):
            rows = slice(blk * cr, (blk + 1) * cr)
            for tri, cols in ((t_fwd, slice(0, GLA_K_W)), (t_bwd, slice(GLA_K_W, 2 * GLA_K_W))):
                gp = g[rows, cols]
                hi = gp.astype(BF16)
                lo = (gp - hi.astype(F32)).astype(BF16)
                b = _dot(tri, hi) + _dot(tri, lo)
                bdec_ref[rows, cols] = b
                b_min = b if b_min is None else jnp.minimum(b_min, b)
        b_min = jnp.min(jnp.min(b_min, axis=0, keepdims=True), axis=1, keepdims=True)
        bmin_ref[...] = jnp.broadcast_to(b_min, bmin_ref.shape)

    def project(c0, c1, out_ref, finish):
        for a in range(c0, c1, PROJ_COL_CHUNK):
            out_ref[:, a - c0:a - c0 + PROJ_COL_CHUNK] = finish(_dot(hb, wm_ref[:, a:a + PROJ_COL_CHUNK])).astype(BF16)

    decay_gates()
    kp = _dot(hb, wm_ref[:, C_AK:C_AV])
    knw = knw_ref[...]
    for i in range(ATTN_KV_HEADS):
        sl = slice(i * HEAD_DIM, (i + 1) * HEAD_DIM)
        kv_ref[:, sl] = norm_rope(kp[:, sl], knw).astype(BF16)
    kv_ref[:, ATTN_KV_W:] = _dot(hb, wm_ref[:, C_AV:C_GK]).astype(BF16)
    project(C_GK, C_GV, gk_ref, lambda t: t)
    project(C_GV, C_AQ, gv_ref, lambda t: t)
    qnw = qnw_ref[...]
    heads_per_chunk = PROJ_COL_CHUNK // HEAD_DIM
    for c in range(ATTN_HEADS // heads_per_chunk):
        qp = _dot(hb, wm_ref[:, C_AQ + c * PROJ_COL_CHUNK:C_AQ + (c + 1) * PROJ_COL_CHUNK])
        for ih in range(heads_per_chunk):
            i = c * heads_per_chunk + ih
            qh = (norm_rope(qp[:, ih * HEAD_DIM:(ih + 1) * HEAD_DIM], qnw) * QK_SCALE_LOG2).astype(BF16)
            for blk in range(tm // ATTN_BLOCK):
                dst = (blk * ATTN_HEADS + i) * ATTN_BLOCK
                q_ref[dst:dst + ATTN_BLOCK, :] = qh[blk * ATTN_BLOCK:(blk + 1) * ATTN_BLOCK, :]
    project(C_GQ, C_GR, gq_ref, lambda t: t * (GLA_DK ** -0.5))
    project(C_GR, C_GA, sr_ref, lambda t: t * _sigmoid(t))
    project(C_GA, C_GG, sga_ref, _sigmoid)
    project(C_GG, C_END, sgg_ref, _sigmoid)


def _proj(lay, layer, x_all, mod, norm_w, wm, wlr, gw, gb, qnw, knw, rope_c, rope_s):
    tm, n = lay.tm, lay.n
    row = lambda w: pl.BlockSpec((tm, w), lambda i: (i, 0))
    widths = [2 * ATTN_KV_W, GLA_K_W, GLA_K_W, GLA_V_W, 2 * GLA_K_W, GLA_V_W, D_MODEL, D_MODEL]
    dtypes = [BF16, BF16, BF16, BF16, F32, BF16, BF16, BF16]
    q_spec = pl.BlockSpec((tm * ATTN_HEADS, HEAD_DIM), lambda i: (i, 0))
    q_shape = jax.ShapeDtypeStruct((n * ATTN_HEADS, HEAD_DIM), BF16)
    return pl.pallas_call(
        _proj_kernel,
        grid=(lay.tiles,),
        in_specs=[
            row(D_MODEL),
            pl.BlockSpec((None, None, 1, N_MOD * D_MODEL), lambda i: (layer, lay.mod_row(i), 0, 0)),
            pl.BlockSpec((None, 3, D_MODEL), lambda i: (layer, 0, 0)),
            _resident((None, D_MODEL, C_END), lambda i: (layer, 0, 0)),
            _resident((None, D_MODEL, LR_PAD), lambda i: (layer, 0, 0)),
            _resident((None, LR_PAD, 2 * GLA_K_W), lambda i: (layer, 0, 0)),
            pl.BlockSpec((None, 1, 2 * GLA_K_W), lambda i: (layer, 0, 0)),
            pl.BlockSpec((None, 1, HEAD_DIM), lambda i: (layer, 0, 0)),
            pl.BlockSpec((None, 1, HEAD_DIM), lambda i: (layer, 0, 0)),
            pl.BlockSpec((tm, HEAD_DIM), lambda i: (lay.rope_block(i), 0)),
            pl.BlockSpec((tm, HEAD_DIM), lambda i: (lay.rope_block(i), 0)),
        ],
        out_specs=[q_spec] + [row(w) for w in widths] + [pl.BlockSpec((SUBLANES, LANES), lambda i: (i, 0))],
        out_shape=([q_shape] + [jax.ShapeDtypeStruct((n, w), dt) for w, dt in zip(widths, dtypes)]
                   + [jax.ShapeDtypeStruct((lay.tiles * SUBLANES, LANES), F32)]),
        compiler_params=_params("arbitrary"),
    )(x_all, mod, norm_w, wm, wlr, gw, gb, qnw, knw, rope_c, rope_s)


def _rope_tables(lay):
    pos = np.arange(lay.s)
    half = HEAD_DIM // 2
    inv_freq = ROPE_THETA ** (-np.arange(0, half, 2, dtype=np.float32) / half)
    inv_freq = jnp.asarray(inv_freq, F32)
    ang_r = jnp.asarray(pos // GRID_W, F32)[:, None] * inv_freq[None, :]
    ang_c = jnp.asarray(pos % GRID_W, F32)[:, None] * inv_freq[None, :]
    cos = jnp.concatenate([jnp.cos(ang_r)] * 2 + [jnp.cos(ang_c)] * 2, axis=-1)
    sin = jnp.concatenate([-jnp.sin(ang_r), jnp.sin(ang_r), -jnp.sin(ang_c), jnp.sin(ang_c)], axis=-1)
    cos = jnp.concatenate([jnp.ones((lay.tm, HEAD_DIM), F32), cos], axis=0)
    sin = jnp.concatenate([jnp.zeros((lay.tm, HEAD_DIM), F32), sin], axis=0)
    return cos, sin


NEG_BIG = -1e30
ATTN_FIXED_SHIFT_MAX = 50.0
ATTN_BLOCKS_PER_STEP = 2
ATTN_BOUND_SLACK = 1.02


def _attn_kernel(sink_ref, bound_ref, q_ref, kp_ref, kc_ref, kn_ref, kx_ref, o_ref, kcat_ref, vcat_ref, *,
                 layer, ctx_steps, lat_steps, ctx_len, nb):
    j = pl.program_id(1)
    jl = j - ctx_steps
    qb = ATTN_BLOCK
    rows = ATTN_GROUP * qb
    win = 3 * qb
    grp = lax.broadcasted_iota(jnp.int32, (rows, 1), 0) // qb

    def sink_column(kvh):
        sink = jnp.zeros((rows, 1), F32)
        for g in range(ATTN_GROUP):
            sink = jnp.where(grp == g, sink_ref[layer, kvh * ATTN_GROUP + g] * LOG2_E, sink)
        return sink

    bound = bound_ref[layer]
    small = bound <= ATTN_FIXED_SHIFT_MAX

    def q_rows(u, kvh):
        start = (u * ATTN_KV_HEADS + kvh) * rows
        return q_ref[start:start + rows, :]

    def finish(u, kvh, s, v, fixed_shift):
        sink = sink_column(kvh)
        m = jnp.maximum(bound, sink) if fixed_shift else jnp.maximum(jnp.max(s, axis=-1, keepdims=True), sink)
        p = jnp.exp2(s - m)
        den = jnp.sum(p, axis=-1, keepdims=True) + jnp.exp2(sink - m)
        o = _dot(p.astype(BF16), v) / den
        for g in range(ATTN_GROUP):
            hd = kvh * ATTN_GROUP + g
            o_ref[u * qb:(u + 1) * qb, hd * HEAD_DIM:(hd + 1) * HEAD_DIM] = o[g * qb:(g + 1) * qb].astype(o_ref.dtype)

    def context_queries(fixed_shift):
        for u in range(nb):
            for kvh in range(ATTN_KV_HEADS):
                k = kx_ref[:, kvh * HEAD_DIM:(kvh + 1) * HEAD_DIM]
                v = kx_ref[:, ATTN_KV_W + kvh * HEAD_DIM:ATTN_KV_W + (kvh + 1) * HEAD_DIM]
                finish(u, kvh, _dot_nt(q_rows(u, kvh), k), v, fixed_shift)

    def window_piece(p):
        if p < 0:
            return kp_ref, (nb + p) * qb, jl > 0
        if p >= nb:
            return kn_ref, (p - nb) * qb, jl < lat_steps - 1
        return kc_ref, p * qb, None

    def latent_queries(fixed_shift):
        row_i = lax.broadcasted_iota(jnp.int32, (rows, qb), 0) % qb
        col_i = lax.broadcasted_iota(jnp.int32, (rows, qb), 1)
        for u in range(nb):
            pieces = [window_piece(u + d) for d in (-1, 0, 1)]
            see_prev = col_i >= row_i if pieces[0][2] is None else (col_i >= row_i) & pieces[0][2]
            see_next = col_i <= row_i if pieces[2][2] is None else (col_i <= row_i) & pieces[2][2]
            for kvh in range(ATTN_KV_HEADS):
                ksl = slice(kvh * HEAD_DIM, (kvh + 1) * HEAD_DIM)
                vsl = slice(ATTN_KV_W + kvh * HEAD_DIM, ATTN_KV_W + (kvh + 1) * HEAD_DIM)
                for idx, (ref, r0, _) in enumerate(pieces):
                    kcat_ref[idx * qb:(idx + 1) * qb, :] = ref[r0:r0 + qb, ksl]
                    vcat_ref[idx * qb:(idx + 1) * qb, :] = ref[r0:r0 + qb, vsl]
                kcat_ref[win:win + ctx_len, :] = kx_ref[:, ksl]
                vcat_ref[win:win + ctx_len, :] = kx_ref[:, vsl]
                s = _dot_nt(q_rows(u, kvh), kcat_ref[...])
                s = jnp.concatenate([jnp.where(see_prev, s[:, :qb], NEG_BIG), s[:, qb:2 * qb],
                                     jnp.where(see_next, s[:, 2 * qb:win], NEG_BIG), s[:, win:]], axis=1)
                finish(u, kvh, s, vcat_ref[...], fixed_shift)

    is_ctx = j < ctx_steps
    pl.when(is_ctx & small)(lambda: context_queries(True))
    pl.when(is_ctx & jnp.logical_not(small))(lambda: context_queries(False))
    pl.when(jnp.logical_not(is_ctx) & small)(lambda: latent_queries(True))
    pl.when(jnp.logical_not(is_ctx) & jnp.logical_not(small))(lambda: latent_queries(False))


def _attn(lay, layer, sink, bound, q, kv):
    nb = ATTN_BLOCKS_PER_STEP
    while lay.l % (nb * ATTN_BLOCK) or lay.s % (nb * ATTN_BLOCK):
        nb //= 2
    qb = ATTN_BLOCK
    span = nb * qb
    cb, lb = lay.l // span, lay.s // span
    lat0 = lay.n_ctx // span

    def q_blk(b, j):
        return jnp.where(j < cb, b * cb + j, lat0 + b * lb + (j - cb))

    def win_blk(delta):
        def f(b, j):
            return (lat0 + b * lb + jnp.clip(j - cb + delta, 0, lb - 1), 0)
        return f

    return pl.pallas_call(
        functools.partial(_attn_kernel, layer=layer, ctx_steps=cb, lat_steps=lb, ctx_len=lay.l, nb=nb),
        grid=(lay.b, cb + lb),
        in_specs=[
            pl.BlockSpec(memory_space=pltpu.SMEM),
            pl.BlockSpec(memory_space=pltpu.SMEM),
            pl.BlockSpec((span * ATTN_HEADS, HEAD_DIM), lambda b, j: (q_blk(b, j), 0)),
            pl.BlockSpec((span, 2 * ATTN_KV_W), win_blk(-1)),
            pl.BlockSpec((span, 2 * ATTN_KV_W), win_blk(0)),
            pl.BlockSpec((span, 2 * ATTN_KV_W), win_blk(1)),
            pl.BlockSpec((lay.l, 2 * ATTN_KV_W), lambda b, j: (b, 0)),
        ],
        out_specs=pl.BlockSpec((span, ATTN_Q_W), lambda b, j: (q_blk(b, j), 0)),
        out_shape=jax.ShapeDtypeStruct((lay.n, ATTN_Q_W), BF16),
        scratch_shapes=[
            pltpu.VMEM((3 * qb + lay.l, HEAD_DIM), BF16),
            pltpu.VMEM((3 * qb + lay.l, HEAD_DIM), BF16),
        ],
        compiler_params=_params("arbitrary", "arbitrary"),
    )(sink, bound, q, kv, kv, kv, kv)


class _GlaJob(NamedTuple):
    q_ref: object
    k_ref: object
    v_ref: object
    b_ref: object
    o_ref: object
    s_ref: object
    row0: int
    backward: bool


class _GlaOperands(NamedTuple):
    rows: slice
    b: jax.Array
    q: jax.Array
    k: jax.Array
    v: jax.Array
    qd: jax.Array
    kd: object
    k_end: jax.Array
    decay_end: jax.Array
    visible: jax.Array


def _gla_chunks(jobs, kf_ref, bf_ref, *, fast):
    ch = GLA_CHUNK
    ri = lax.broadcasted_iota(jnp.int32, (ch, ch), 0)
    ci = lax.broadcasted_iota(jnp.int32, (ch, ch), 1)
    diag = lax.broadcasted_iota(jnp.int32, (GLA_DK, GLA_DK), 0) == lax.broadcasted_iota(jnp.int32, (GLA_DK, GLA_DK), 1)
    ksls = [slice(hd * GLA_DK, (hd + 1) * GLA_DK) for hd in range(GLA_HEADS)]
    vsls = [slice(hd * GLA_DV, (hd + 1) * GLA_DV) for hd in range(GLA_HEADS)]
    heads = range(GLA_HEADS)

    def operands(job):
        rows = slice(job.row0, job.row0 + ch)
        b = job.b_ref[rows, :]
        q = job.q_ref[rows, :]
        k = job.k_ref[rows, :]
        end = 0 if job.backward else ch - 1
        b_end = b[end:end + 1, :]
        return _GlaOperands(
            rows=rows, b=b, q=q, k=k, v=job.v_ref[rows, :],
            qd=q * jnp.exp2(b).astype(BF16),
            kd=k * jnp.exp2(-b).astype(BF16) if fast else None,
            k_end=k * jnp.exp2(b_end - b).astype(BF16),
            decay_end=jnp.broadcast_to(jnp.exp2(b_end), (GLA_DK, GLA_K_W)),
            visible=(ci >= ri) if job.backward else (ci <= ri))

    def exact_scores(x, ksl):
        kf_ref[...] = x.k.astype(F32)
        bf_ref[...] = x.b
        q_h, b_h = x.q.astype(F32)[:, ksl], x.b[:, ksl]

        def columns(grp, a_acc):
            base = pl.multiple_of(grp * SUBLANES, SUBLANES)
            k_rows = kf_ref[pl.ds(base, SUBLANES), ksl]
            b_rows = bf_ref[pl.ds(base, SUBLANES), ksl]
            for r in range(SUBLANES):
                decay = jnp.exp2(jnp.minimum(b_h - b_rows[r:r + 1, :], 0.0))
                col = jnp.sum(q_h * k_rows[r:r + 1, :] * decay, axis=-1, keepdims=True)
                a_acc = jnp.where(ci == base + r, col, a_acc)
            return a_acc

        return lax.fori_loop(0, ch // SUBLANES, columns, jnp.zeros((ch, ch), F32))

    ops = [operands(job) for job in jobs]
    if fast:
        scores = [[_dot_nt(x.qd[:, ksls[hd]], x.kd[:, ksls[hd]]) for hd in heads] for x in ops]
    else:
        scores = [[exact_scores(x, ksls[hd]) for hd in heads] for x in ops]
    inter = [[_dot(x.qd[:, ksls[hd]], job.s_ref[hd].astype(BF16)) for hd in heads] for job, x in zip(jobs, ops)]
    lhs = [[jnp.concatenate([jnp.where(x.visible, a, 0.0).astype(BF16), x.k_end[:, ksls[hd]].T], axis=0)
            for hd, a in zip(heads, row)] for x, row in zip(ops, scores)]
    both = [[_dot(lhs_h, x.v[:, vsls[hd]]) for hd, lhs_h in zip(heads, row)] for x, row in zip(ops, lhs)]
    for job, x, inter_j, both_j in zip(jobs, ops, inter, both):
        for hd in heads:
            job.o_ref[x.rows, vsls[hd]] = (inter_j[hd] + both_j[hd][:ch]).astype(job.o_ref.dtype)
            decay_col = jnp.sum(jnp.where(diag, x.decay_end[:, ksls[hd]], 0.0), axis=-1, keepdims=True)
            job.s_ref[hd] = job.s_ref[hd] * decay_col + both_j[hd][ch:]


def _gla_kernel(qf_ref, kf_ref, vf_ref, bf_ref, mf_ref, qb_ref, kb_ref, vb_ref, bb_ref, mb_ref, of_ref, ob_ref,
                sf_ref, sb_ref, ktmp_ref, btmp_ref, *, tb):
    @pl.when(pl.program_id(1) == 0)
    def _():
        sf_ref[...] = jnp.zeros_like(sf_ref)
        sb_ref[...] = jnp.zeros_like(sb_ref)

    n_sub = tb // GLA_CHUNK
    both = jnp.minimum(mf_ref[...], mb_ref[...])
    min_b = jnp.min(jnp.min(both, axis=0, keepdims=True), axis=1, keepdims=True)[0, 0]

    def run(fast):
        for s in range(n_sub):
            jobs = [_GlaJob(qf_ref, kf_ref, vf_ref, bf_ref, of_ref, sf_ref, s * GLA_CHUNK, False),
                    _GlaJob(qb_ref, kb_ref, vb_ref, bb_ref, ob_ref, sb_ref, (n_sub - 1 - s) * GLA_CHUNK, True)]
            _gla_chunks(jobs, ktmp_ref, btmp_ref, fast=fast)

    pl.when(min_b >= GLA_SAFE_LOG_DECAY)(lambda: run(True))
    pl.when(min_b < GLA_SAFE_LOG_DECAY)(lambda: run(False))


def _gla(lay, gq, gk, gv, bdec, bmin):
    tb = lay.tb
    assert lay.tm % tb == 0
    tl, ts = lay.l // tb, lay.s // tb
    lat0 = lay.n_ctx // tb

    def fwd_blk(b, t):
        return jnp.where(t < tl, b * tl + t, lat0 + b * ts + (t - tl))

    def bwd_blk(b, t):
        return jnp.where(t < tl, b * tl + (tl - 1 - t), lat0 + b * ts + (ts - 1 - (t - tl)))

    def specs(blk, half):
        return [
            pl.BlockSpec((tb, GLA_K_W), lambda b, t: (blk(b, t), 0)),
            pl.BlockSpec((tb, GLA_K_W), lambda b, t: (blk(b, t), 0)),
            pl.BlockSpec((tb, GLA_V_W), lambda b, t: (blk(b, t), 0)),
            pl.BlockSpec((tb, GLA_K_W), lambda b, t: (blk(b, t), half)),
            pl.BlockSpec((SUBLANES, LANES), lambda b, t: (blk(b, t) // (lay.tm // tb), 0)),
        ]

    return pl.pallas_call(
        functools.partial(_gla_kernel, tb=tb),
        grid=(lay.b, tl + ts),
        in_specs=specs(fwd_blk, 0) + specs(bwd_blk, 1),
        out_specs=[
            pl.BlockSpec((tb, GLA_V_W), lambda b, t: (fwd_blk(b, t), 0)),
            pl.BlockSpec((tb, GLA_V_W), lambda b, t: (bwd_blk(b, t), 0)),
        ],
        out_shape=[jax.ShapeDtypeStruct((lay.n, GLA_V_W), BF16)] * 2,
        scratch_shapes=[
            pltpu.VMEM((GLA_HEADS, GLA_DK, GLA_DV), F32),
            pltpu.VMEM((GLA_HEADS, GLA_DK, GLA_DV), F32),
            pltpu.VMEM((GLA_CHUNK, GLA_K_W), F32),
            pltpu.VMEM((GLA_CHUNK, GLA_K_W), F32),
        ],
        compiler_params=_params("arbitrary", "arbitrary"),
    )(gq, gk, gv, bdec, bmin, gq, gk, gv, bdec, bmin)


def _merge_kernel(x_ref, mod_ref, ya_ref, of_ref, ob_ref, sr_ref, sga_ref, sgg_ref, gnw_ref, wba_ref, wbg_ref, wo_ref,
                  o_ref):
    o = of_ref[...].astype(F32) + ob_ref[...].astype(F32)
    gnw = gnw_ref[...]
    parts = []
    for hd in range(GLA_HEADS):
        oh = o[:, hd * GLA_DV:(hd + 1) * GLA_DV]
        parts.append(oh * lax.rsqrt(jnp.mean(oh * oh, axis=-1, keepdims=True) + NORM_EPS) * gnw)
    y_gla = (jnp.concatenate(parts, axis=-1) * sr_ref[...].astype(F32)).astype(BF16)
    z = (sga_ref[...].astype(F32) * _dot(ya_ref[...], wba_ref[...])
         + sgg_ref[...].astype(F32) * _dot(y_gla, wbg_ref[...]))
    y = _dot(z.astype(BF16), wo_ref[...])
    o_ref[...] = x_ref[...] + _mod_slice(mod_ref, 5) * y


def _merge(lay, layer, x_all, mod, y_attn, o_f, o_b, sr, sga, sgg, gnw, wba, wbg, wo):
    tm = lay.tm
    row = pl.BlockSpec((tm, D_MODEL), lambda i: (i, 0))
    weight = _resident((None, D_MODEL, D_MODEL), lambda i: (layer, 0, 0))
    return pl.pallas_call(
        _merge_kernel,
        grid=(lay.tiles,),
        in_specs=[
            row,
            pl.BlockSpec((None, None, 1, N_MOD * D_MODEL), lambda i: (layer, lay.mod_row(i), 0, 0)),
            row, row, row, row, row, row,
            pl.BlockSpec((None, 1, GLA_DV), lambda i: (layer, 0, 0)),
            weight, weight, weight,
        ],
        out_specs=row,
        out_shape=jax.ShapeDtypeStruct((lay.n, D_MODEL), F32),
        compiler_params=_params("arbitrary"),
    )(x_all, mod, y_attn, o_f, o_b, sr, sga, sgg, gnw, wba, wbg, wo)


def kernel(x, c, ctx, c_ctx, w_mod, b_mod, norm_w, ffn1_w13, ffn1_w2, ffn2_w13, ffn2_w2, w_in, q_norm_w, k_norm_w,
           attn_sink, gla_gate_w_fwd, gla_gate_b_fwd, gla_gate_w_bwd, gla_gate_b_bwd, gla_norm_w, w_branch_attn,
           w_branch_gla, w_out):
    batch, seq, _ = x.shape
    ctx_len = ctx.shape[1]
    depth = w_mod.shape[0]
    assert batch + 1 <= MOD_ROWS
    lay = _Layout(batch, seq, ctx_len)
    lay_ffn = _Layout(batch, seq, ctx_len, FFN_TM_MAX)

    cond_rows = jnp.concatenate([c, c_ctx[None, :], jnp.zeros((MOD_ROWS - batch - 1, D_MODEL), F32)], axis=0)
    ctx_cols = 2 * ATTN_KV_W + GLA_K_W + GLA_V_W
    lr0, lr1 = ctx_cols, ctx_cols + 2 * GLA_RANK
    wm = jnp.concatenate([w_in[:, :, :lr0], w_in[:, :, lr1:]], axis=-1).astype(BF16)
    wlr = jnp.pad(w_in[:, :, lr0:lr1], ((0, 0), (0, 0), (0, LR_PAD - 2 * GLA_RANK))).astype(BF16)
    gw = jnp.zeros((depth, LR_PAD, 2 * GLA_K_W), F32)
    gw = gw.at[:, :GLA_RANK, :GLA_K_W].set(gla_gate_w_fwd).at[:, GLA_RANK:2 * GLA_RANK, GLA_K_W:].set(gla_gate_w_bwd)
    gw = gw.astype(BF16)
    gb = jnp.concatenate([gla_gate_b_fwd, gla_gate_b_bwd], axis=-1)[:, None, :]
    w13_1, w2_1 = ffn1_w13.astype(BF16), ffn1_w2.astype(BF16)
    w13_2, w2_2 = ffn2_w13.astype(BF16), ffn2_w2.astype(BF16)
    wba, wbg, wo = w_branch_attn.astype(BF16), w_branch_gla.astype(BF16), w_out.astype(BF16)
    qnw, knw, gnw = q_norm_w[:, None, :], k_norm_w[:, None, :], gla_norm_w[:, None, :]
    rope_c, rope_s = _rope_tables(lay)
    logit_bound = (HEAD_DIM * QK_SCALE_LOG2 * ATTN_BOUND_SLACK) * (
        jnp.max(jnp.abs(q_norm_w), axis=-1) * jnp.max(jnp.abs(k_norm_w), axis=-1))

    mod = _modulation(cond_rows, w_mod, b_mod).reshape(depth, MOD_ROWS, 1, N_MOD * D_MODEL)
    x_all = (ctx.reshape(batch * ctx_len, D_MODEL), x.reshape(batch * seq, D_MODEL))

    for layer in range(depth):
        last = layer == depth - 1
        x_all = _ffn(lay_ffn, layer, 0, x_all, mod, norm_w, w13_1, w2_1)
        q, kv, gq, gk, gv, bdec, sr, sga, sgg, bmin = _proj(lay, layer, x_all, mod, norm_w, wm, wlr, gw, gb, qnw, knw,
                                                       rope_c, rope_s)
        y_attn = _attn(lay, layer, attn_sink, logit_bound, q, kv)
        o_f, o_b = _gla(lay, gq, gk, gv, bdec, bmin)
        x_all = _merge(lay, layer, x_all, mod, y_attn, o_f, o_b, sr, sga, sgg, gnw, wba, wbg, wo)
        x_all = _ffn(lay_ffn, layer, 2, x_all, mod, norm_w, w13_2, w2_2, latents_only=last)
    return x_all.reshape(batch, seq, D_MODEL)
```

```python
import functools
from typing import NamedTuple

import numpy as np
import jax
import jax.numpy as jnp
from jax import lax
from jax.experimental import pallas as pl
from jax.experimental.pallas import tpu as pltpu

F32 = jnp.float32
BF16 = jnp.bfloat16

D_MODEL = 1024
N_MOD = 9
NORM_EPS = 1e-6
D_FF = 2816
GRID_W = 64
ATTN_HEADS = 8
ATTN_KV_HEADS = 2
ATTN_GROUP = ATTN_HEADS // ATTN_KV_HEADS
HEAD_DIM = 128
ATTN_BLOCK = 128
ROPE_THETA = 10000.0
LOG2_E = 1.4426950408889634
QK_SCALE_LOG2 = HEAD_DIM ** -0.5 * LOG2_E
GLA_HEADS = 4
GLA_DK = 128
GLA_DV = 256
GLA_RANK = 16
GLA_TEMP = 16.0
GLA_CHUNK = 256
ATTN_Q_W = ATTN_HEADS * HEAD_DIM
ATTN_KV_W = ATTN_KV_HEADS * HEAD_DIM
GLA_K_W = GLA_HEADS * GLA_DK
GLA_V_W = GLA_HEADS * GLA_DV
GLA_SAFE_LOG_DECAY = -60.0 * LOG2_E

V7X_VMEM_BYTES = 64 * 1024 * 1024
VMEM_LIMIT = V7X_VMEM_BYTES - 8 * 1024 * 1024
MOD_ROWS = 16
MOD_COL_TILE = 1024
SUBLANES = 8
LANES = 128


def _dot(a, b):
    return jnp.dot(a, b, preferred_element_type=F32)


def _dot_nt(a, b):
    return lax.dot_general(a, b, (((1,), (1,)), ((), ())), preferred_element_type=F32)


def _dot_tn(a, b):
    return lax.dot_general(a, b, (((0,), (0,)), ((), ())), preferred_element_type=F32)


def _sigmoid(x):
    return 1.0 / (1.0 + jnp.exp(-x))


def _resident(block_shape, index_map):
    return pl.BlockSpec(block_shape, index_map, pipeline_mode=pl.Buffered(1))


def _params(*sem):
    return pltpu.CompilerParams(dimension_semantics=sem, vmem_limit_bytes=VMEM_LIMIT)


class _Layout:
    def __init__(self, batch, seq, ctx_len, tm_max=512):
        self.b, self.s, self.l = batch, seq, ctx_len
        self.n_ctx = batch * ctx_len
        self.n = self.n_ctx + batch * seq
        tm = tm_max
        while self.n_ctx % tm or seq % tm:
            tm //= 2
        assert tm >= GLA_CHUNK and tm % GLA_CHUNK == 0
        self.tm = tm
        self.ctx_tiles = self.n_ctx // tm
        self.tiles_per_batch = seq // tm
        self.tiles = self.n // tm
        tb = 256
        while ctx_len % tb or seq % tb:
            tb //= 2
        assert tb >= GLA_CHUNK
        self.tb = tb
        assert ctx_len % ATTN_BLOCK == 0 and seq % ATTN_BLOCK == 0 and seq % GRID_W == 0

    def mod_row(self, i):
        return jnp.where(i < self.ctx_tiles, self.b, (i - self.ctx_tiles) // self.tiles_per_batch)

    def rope_block(self, i):
        return jnp.where(i < self.ctx_tiles, 0, 1 + (i - self.ctx_tiles) % self.tiles_per_batch)


def _mod_kernel(c_ref, w_ref, b_ref, o_ref):
    c = c_ref[...]
    cond = (c * _sigmoid(c)).astype(BF16)
    o_ref[...] = _dot(cond, w_ref[...].astype(BF16)) + b_ref[...]


def _modulation(cond_rows, w_mod, b_mod):
    depth = w_mod.shape[0]
    n_cols = N_MOD * D_MODEL
    return pl.pallas_call(
        _mod_kernel,
        grid=(depth, n_cols // MOD_COL_TILE),
        in_specs=[
            pl.BlockSpec((MOD_ROWS, D_MODEL), lambda l, j: (0, 0)),
            pl.BlockSpec((None, D_MODEL, MOD_COL_TILE), lambda l, j: (l, 0, j)),
            pl.BlockSpec((None, 1, MOD_COL_TILE), lambda l, j: (l, 0, j)),
        ],
        out_specs=pl.BlockSpec((None, MOD_ROWS, MOD_COL_TILE), lambda l, j: (l, 0, j)),
        out_shape=jax.ShapeDtypeStruct((depth, MOD_ROWS, n_cols), F32),
        compiler_params=_params("arbitrary", "arbitrary"),
    )(cond_rows, w_mod, b_mod.reshape(depth, 1, n_cols))


def _mod_slice(mod_ref, k):
    return mod_ref[:, k * D_MODEL:(k + 1) * D_MODEL]


def _norm_modulate(x, nw, shift, scale):
    y = x * lax.rsqrt(jnp.mean(x * x, axis=-1, keepdims=True) + NORM_EPS) * nw
    return y * (1.0 + scale) + shift


FFN_CHUNKS = 11
FFN_TM_MAX = 1024


def _ffn_kernel(x_ref, mod_ref, nw_ref, w13_ref, w2_ref, o_ref, *, sub):
    _ffn_body(x_ref[...], mod_ref, nw_ref, w13_ref, w2_ref, o_ref, sub)


def _ffn_first_kernel(xc_ref, xl_ref, mod_ref, nw_ref, w13_ref, w2_ref, o_ref, *, sub, ctx_tiles):
    x = jnp.where(pl.program_id(0) < ctx_tiles, xc_ref[...], xl_ref[...])
    _ffn_body(x, mod_ref, nw_ref, w13_ref, w2_ref, o_ref, sub)


def _ffn_body(x, mod_ref, nw_ref, w13_ref, w2_ref, o_ref, sub):
    h = _norm_modulate(x, nw_ref[sub:sub + 1, :], _mod_slice(mod_ref, 3 * sub), _mod_slice(mod_ref, 3 * sub + 1))
    hb = h.astype(BF16)
    fc = D_FF // FFN_CHUNKS
    acc = None
    for c in range(FFN_CHUNKS):
        up = _dot(hb, w13_ref[:, c * fc:(c + 1) * fc])
        gate = _dot(hb, w13_ref[:, D_FF + c * fc:D_FF + (c + 1) * fc])
        a = (gate * _sigmoid(gate) * up).astype(BF16)
        part = _dot(a, w2_ref[c * fc:(c + 1) * fc, :])
        acc = part if acc is None else acc + part
    o_ref[...] = x + (0.5 * _mod_slice(mod_ref, 3 * sub + 2)) * acc


def _ffn(lay, layer, sub, x_rows, mod, norm_w, w13, w2, *, latents_only=False):
    tm = lay.tm
    off = lay.ctx_tiles if latents_only else 0
    n_tiles = lay.tiles - off
    if isinstance(x_rows, tuple):
        assert not latents_only
        body = functools.partial(_ffn_first_kernel, sub=sub, ctx_tiles=lay.ctx_tiles)
        row_specs = [
            pl.BlockSpec((tm, D_MODEL), lambda i: (jnp.minimum(i, lay.ctx_tiles - 1), 0)),
            pl.BlockSpec((tm, D_MODEL), lambda i: (jnp.maximum(i - lay.ctx_tiles, 0), 0)),
        ]
    else:
        body = functools.partial(_ffn_kernel, sub=sub)
        row_specs = [pl.BlockSpec((tm, D_MODEL), lambda i: (i + off, 0))]
        x_rows = (x_rows,)
    return pl.pallas_call(
        body,
        grid=(n_tiles,),
        in_specs=row_specs + [
            pl.BlockSpec((None, None, 1, N_MOD * D_MODEL), lambda i: (layer, lay.mod_row(i + off), 0, 0)),
            pl.BlockSpec((None, 3, D_MODEL), lambda i: (layer, 0, 0)),
            _resident((None, D_MODEL, 2 * D_FF), lambda i: (layer, 0, 0)),
            _resident((None, D_FF, D_MODEL), lambda i: (layer, 0, 0)),
        ],
        out_specs=pl.BlockSpec((tm, D_MODEL), lambda i: (i, 0)),
        out_shape=jax.ShapeDtypeStruct((n_tiles * tm, D_MODEL), F32),
        compiler_params=_params("arbitrary"),
    )(*x_rows, mod, norm_w, w13, w2)


C_AK, C_AV, C_GK, C_GV, C_AQ, C_GQ, C_GR, C_GA, C_GG, C_END = np.cumsum(
    [0, ATTN_KV_W, ATTN_KV_W, GLA_K_W, GLA_V_W, ATTN_Q_W, GLA_K_W, GLA_V_W, D_MODEL, D_MODEL]).tolist()
LR_PAD = 128
CUMSUM_ROWS = 256
PROJ_COL_CHUNK = 256


def _log_sigmoid(x):
    return jnp.minimum(x, 0.0) - jnp.log(1.0 + jnp.exp(-jnp.abs(x)))


def _proj_kernel(x_ref, mod_ref, nw_ref, wm_ref, wlr_ref, gw_ref, gb_ref, qnw_ref, knw_ref, rc_ref, rs_ref,
                 q_ref, kv_ref, gq_ref, gk_ref, gv_ref, bdec_ref, sr_ref, sga_ref, sgg_ref, bmin_ref):
    x = x_ref[...]
    tm = x.shape[0]
    h = _norm_modulate(x, nw_ref[1:2, :], _mod_slice(mod_ref, 3), _mod_slice(mod_ref, 4))
    hb = h.astype(BF16)
    rope_c = rc_ref[...]
    rope_s = rs_ref[...]
    lane = lax.broadcasted_iota(jnp.int32, (tm, HEAD_DIM), 1)
    first_half = (lane % (HEAD_DIM // 2)) < (HEAD_DIM // 4)

    def norm_rope(t, w):
        y = t * lax.rsqrt(jnp.mean(t * t, axis=-1, keepdims=True) + NORM_EPS) * w
        partner = jnp.where(first_half, pltpu.roll(y, HEAD_DIM - HEAD_DIM // 4, 1), pltpu.roll(y, HEAD_DIM // 4, 1))
        return y * rope_c + partner * rope_s

    def decay_gates():
        lr = _dot(hb, wlr_ref[...]).astype(BF16)
        g = _log_sigmoid(_dot(lr, gw_ref[...]) + gb_ref[...]) * (LOG2_E / GLA_TEMP)
        cr = min(CUMSUM_ROWS, tm)
        ri = lax.broadcasted_iota(jnp.int32, (cr, cr), 0)
        ci = lax.broadcasted_iota(jnp.int32, (cr, cr), 1)
        same = (ri // GLA_CHUNK) == (ci // GLA_CHUNK)
        t_fwd = jnp.where(same & (ci <= ri), 1.0, 0.0).astype(BF16)
        t_bwd = jnp.where(same & (ci >= ri), 1.0, 0.0).astype(BF16)
        b_min = None
        for blk in range(tm // cr):
            rows = slice(blk * cr, (blk + 1) * cr)
            for tri, cols in ((t_fwd, slice(0, GLA_K_W)), (t_bwd, slice(GLA_K_W, 2 * GLA_K_W))):
                gp = g[rows, cols]
                hi = gp.astype(BF16)
                lo = (gp - hi.astype(F32)).astype(BF16)
                b = _dot(tri, hi) + _dot(tri, lo)
                bdec_ref[rows, cols] = b
                b_min = b if b_min is None else jnp.minimum(b_min, b)
        b_min = jnp.min(jnp.min(b_min, axis=0, keepdims=True), axis=1, keepdims=True)
        bmin_ref[...] = jnp.broadcast_to(b_min, bmin_ref.shape)

    def project(c0, c1, out_ref, finish):
        for a in range(c0, c1, PROJ_COL_CHUNK):
            out_ref[:, a - c0:a - c0 + PROJ_COL_CHUNK] = finish(_dot(hb, wm_ref[:, a:a + PROJ_COL_CHUNK])).astype(BF16)

    decay_gates()
    kp = _dot(hb, wm_ref[:, C_AK:C_AV])
    knw = knw_ref[...]
    for i in range(ATTN_KV_HEADS):
        sl = slice(i * HEAD_DIM, (i + 1) * HEAD_DIM)
        kv_ref[:, sl] = norm_rope(kp[:, sl], knw).astype(BF16)
    kv_ref[:, ATTN_KV_W:] = _dot(hb, wm_ref[:, C_AV:C_GK]).astype(BF16)
    project(C_GK, C_GV, gk_ref, lambda t: t)
    project(C_GV, C_AQ, gv_ref, lambda t: t)
    qnw = qnw_ref[...]
    heads_per_chunk = PROJ_COL_CHUNK // HEAD_DIM
    for c in range(ATTN_HEADS // heads_per_chunk):
        qp = _dot(hb, wm_ref[:, C_AQ + c * PROJ_COL_CHUNK:C_AQ + (c + 1) * PROJ_COL_CHUNK])
        for ih in range(heads_per_chunk):
            i = c * heads_per_chunk + ih
            qh = (norm_rope(qp[:, ih * HEAD_DIM:(ih + 1) * HEAD_DIM], qnw) * QK_SCALE_LOG2).astype(BF16)
            for blk in range(tm // ATTN_BLOCK):
                dst = (blk * ATTN_HEADS + i) * ATTN_BLOCK
                q_ref[dst:dst + ATTN_BLOCK, :] = qh[blk * ATTN_BLOCK:(blk + 1) * ATTN_BLOCK, :]
    project(C_GQ, C_GR, gq_ref, lambda t: t * (GLA_DK ** -0.5))
    project(C_GR, C_GA, sr_ref, lambda t: t * _sigmoid(t))
    project(C_GA, C_GG, sga_ref, _sigmoid)
    project(C_GG, C_END, sgg_ref, _sigmoid)


def _proj(lay, layer, x_all, mod, norm_w, wm, wlr, gw, gb, qnw, knw, rope_c, rope_s):
    tm, n = lay.tm, lay.n
    row = lambda w: pl.BlockSpec((tm, w), lambda i: (i, 0))
    widths = [2 * ATTN_KV_W, GLA_K_W, GLA_K_W, GLA_V_W, 2 * GLA_K_W, GLA_V_W, D_MODEL, D_MODEL]
    dtypes = [BF16, BF16, BF16, BF16, F32, BF16, BF16, BF16]
    q_spec = pl.BlockSpec((tm * ATTN_HEADS, HEAD_DIM), lambda i: (i, 0))
    q_shape = jax.ShapeDtypeStruct((n * ATTN_HEADS, HEAD_DIM), BF16)
    return pl.pallas_call(
        _proj_kernel,
        grid=(lay.tiles,),
        in_specs=[
            row(D_MODEL),
            pl.BlockSpec((None, None, 1, N_MOD * D_MODEL), lambda i: (layer, lay.mod_row(i), 0, 0)),
            pl.BlockSpec((None, 3, D_MODEL), lambda i: (layer, 0, 0)),
            _resident((None, D_MODEL, C_END), lambda i: (layer, 0, 0)),
            _resident((None, D_MODEL, LR_PAD), lambda i: (layer, 0, 0)),
            _resident((None, LR_PAD, 2 * GLA_K_W), lambda i: (layer, 0, 0)),
            pl.BlockSpec((None, 1, 2 * GLA_K_W), lambda i: (layer, 0, 0)),
            pl.BlockSpec((None, 1, HEAD_DIM), lambda i: (layer, 0, 0)),
            pl.BlockSpec((None, 1, HEAD_DIM), lambda i: (layer, 0, 0)),
            pl.BlockSpec((tm, HEAD_DIM), lambda i: (lay.rope_block(i), 0)),
            pl.BlockSpec((tm, HEAD_DIM), lambda i: (lay.rope_block(i), 0)),
        ],
        out_specs=[q_spec] + [row(w) for w in widths] + [pl.BlockSpec((SUBLANES, LANES), lambda i: (i, 0))],
        out_shape=([q_shape] + [jax.ShapeDtypeStruct((n, w), dt) for w, dt in zip(widths, dtypes)]
                   + [jax.ShapeDtypeStruct((lay.tiles * SUBLANES, LANES), F32)]),
        compiler_params=_params("arbitrary"),
    )(x_all, mod, norm_w, wm, wlr, gw, gb, qnw, knw, rope_c, rope_s)


def _rope_tables(lay):
    pos = np.arange(lay.s)
    half = HEAD_DIM // 2
    inv_freq = ROPE_THETA ** (-np.arange(0, half, 2, dtype=np.float32) / half)
    inv_freq = jnp.asarray(inv_freq, F32)
    ang_r = jnp.asarray(pos // GRID_W, F32)[:, None] * inv_freq[None, :]
    ang_c = jnp.asarray(pos % GRID_W, F32)[:, None] * inv_freq[None, :]
    cos = jnp.concatenate([jnp.cos(ang_r)] * 2 + [jnp.cos(ang_c)] * 2, axis=-1)
    sin = jnp.concatenate([-jnp.sin(ang_r), jnp.sin(ang_r), -jnp.sin(ang_c), jnp.sin(ang_c)], axis=-1)
    cos = jnp.concatenate([jnp.ones((lay.tm, HEAD_DIM), F32), cos], axis=0)
    sin = jnp.concatenate([jnp.zeros((lay.tm, HEAD_DIM), F32), sin], axis=0)
    return cos, sin


NEG_BIG = -1e30
ATTN_FIXED_SHIFT_MAX = 50.0
ATTN_BLOCKS_PER_STEP = 2
ATTN_BOUND_SLACK = 1.02


def _attn_kernel(sink_ref, bound_ref, q_ref, kp_ref, kc_ref, kn_ref, kx_ref, o_ref, kcat_ref, vcat_ref, *,
                 layer, ctx_steps, lat_steps, ctx_len, nb):
    j = pl.program_id(1)
    jl = j - ctx_steps
    qb = ATTN_BLOCK
    rows = ATTN_GROUP * qb
    win = 3 * qb
    grp = lax.broadcasted_iota(jnp.int32, (rows, 1), 0) // qb

    def sink_column(kvh):
        sink = jnp.zeros((rows, 1), F32)
        for g in range(ATTN_GROUP):
            sink = jnp.where(grp == g, sink_ref[layer, kvh * ATTN_GROUP + g] * LOG2_E, sink)
        return sink

    bound = bound_ref[layer]
    small = bound <= ATTN_FIXED_SHIFT_MAX

    def q_rows(u, kvh):
        start = (u * ATTN_KV_HEADS + kvh) * rows
        return q_ref[start:start + rows, :]

    def finish(u, kvh, s, v, fixed_shift):
        sink = sink_column(kvh)
        m = jnp.maximum(bound, sink) if fixed_shift else jnp.maximum(jnp.max(s, axis=-1, keepdims=True), sink)
        p = jnp.exp2(s - m)
        den = jnp.sum(p, axis=-1, keepdims=True) + jnp.exp2(sink - m)
        o = _dot(p.astype(BF16), v) / den
        for g in range(ATTN_GROUP):
            hd = kvh * ATTN_GROUP + g
            o_ref[u * qb:(u + 1) * qb, hd * HEAD_DIM:(hd + 1) * HEAD_DIM] = o[g * qb:(g + 1) * qb].astype(o_ref.dtype)

    def context_queries(fixed_shift):
        for u in range(nb):
            for kvh in range(ATTN_KV_HEADS):
                k = kx_ref[:, kvh * HEAD_DIM:(kvh + 1) * HEAD_DIM]
                v = kx_ref[:, ATTN_KV_W + kvh * HEAD_DIM:ATTN_KV_W + (kvh + 1) * HEAD_DIM]
                finish(u, kvh, _dot_nt(q_rows(u, kvh), k), v, fixed_shift)

    def window_piece(p):
        if p < 0:
            return kp_ref, (nb + p) * qb, jl > 0
        if p >= nb:
            return kn_ref, (p - nb) * qb, jl < lat_steps - 1
        return kc_ref, p * qb, None

    def latent_queries(fixed_shift):
        row_i = lax.broadcasted_iota(jnp.int32, (rows, qb), 0) % qb
        col_i = lax.broadcasted_iota(jnp.int32, (rows, qb), 1)
        for u in range(nb):
            pieces = [window_piece(u + d) for d in (-1, 0, 1)]
            see_prev = col_i >= row_i if pieces[0][2] is None else (col_i >= row_i) & pieces[0][2]
            see_next = col_i <= row_i if pieces[2][2] is None else (col_i <= row_i) & pieces[2][2]
            for kvh in range(ATTN_KV_HEADS):
                ksl = slice(kvh * HEAD_DIM, (kvh + 1) * HEAD_DIM)
                vsl = slice(ATTN_KV_W + kvh * HEAD_DIM, ATTN_KV_W + (kvh + 1) * HEAD_DIM)
                for idx, (ref, r0, _) in enumerate(pieces):
                    kcat_ref[idx * qb:(idx + 1) * qb, :] = ref[r0:r0 + qb, ksl]
                    vcat_ref[idx * qb:(idx + 1) * qb, :] = ref[r0:r0 + qb, vsl]
                kcat_ref[win:win + ctx_len, :] = kx_ref[:, ksl]
                vcat_ref[win:win + ctx_len, :] = kx_ref[:, vsl]
                s = _dot_nt(q_rows(u, kvh), kcat_ref[...])
                s = jnp.concatenate([jnp.where(see_prev, s[:, :qb], NEG_BIG), s[:, qb:2 * qb],
                                     jnp.where(see_next, s[:, 2 * qb:win], NEG_BIG), s[:, win:]], axis=1)
                finish(u, kvh, s, vcat_ref[...], fixed_shift)

    is_ctx = j < ctx_steps
    pl.when(is_ctx & small)(lambda: context_queries(True))
    pl.when(is_ctx & jnp.logical_not(small))(lambda: context_queries(False))
    pl.when(jnp.logical_not(is_ctx) & small)(lambda: latent_queries(True))
    pl.when(jnp.logical_not(is_ctx) & jnp.logical_not(small))(lambda: latent_queries(False))


def _attn(lay, layer, sink, bound, q, kv):
    nb = ATTN_BLOCKS_PER_STEP
    while lay.l % (nb * ATTN_BLOCK) or lay.s % (nb * ATTN_BLOCK):
        nb //= 2
    qb = ATTN_BLOCK
    span = nb * qb
    cb, lb = lay.l // span, lay.s // span
    lat0 = lay.n_ctx // span

    def q_blk(b, j):
        return jnp.where(j < cb, b * cb + j, lat0 + b * lb + (j - cb))

    def win_blk(delta):
        def f(b, j):
            return (lat0 + b * lb + jnp.clip(j - cb + delta, 0, lb - 1), 0)
        return f

    return pl.pallas_call(
        functools.partial(_attn_kernel, layer=layer, ctx_steps=cb, lat_steps=lb, ctx_len=lay.l, nb=nb),
        grid=(lay.b, cb + lb),
        in_specs=[
            pl.BlockSpec(memory_space=pltpu.SMEM),
            pl.BlockSpec(memory_space=pltpu.SMEM),
            pl.BlockSpec((span * ATTN_HEADS, HEAD_DIM), lambda b, j: (q_blk(b, j), 0)),
            pl.BlockSpec((span, 2 * ATTN_KV_W), win_blk(-1)),
            pl.BlockSpec((span, 2 * ATTN_KV_W), win_blk(0)),
            pl.BlockSpec((span, 2 * ATTN_KV_W), win_blk(1)),
            pl.BlockSpec((lay.l, 2 * ATTN_KV_W), lambda b, j: (b, 0)),
        ],
        out_specs=pl.BlockSpec((span, ATTN_Q_W), lambda b, j: (q_blk(b, j), 0)),
        out_shape=jax.ShapeDtypeStruct((lay.n, ATTN_Q_W), BF16),
        scratch_shapes=[
            pltpu.VMEM((3 * qb + lay.l, HEAD_DIM), BF16),
            pltpu.VMEM((3 * qb + lay.l, HEAD_DIM), BF16),
        ],
        compiler_params=_params("arbitrary", "arbitrary"),
    )(sink, bound, q, kv, kv, kv, kv)


class _GlaJob(NamedTuple):
    q_ref: object
    k_ref: object
    v_ref: object
    b_ref: object
    o_ref: object
    s_ref: object
    row0: int
    backward: bool


class _GlaOperands(NamedTuple):
    rows: slice
    b: jax.Array
    q: jax.Array
    k: jax.Array
    v: jax.Array
    qd: jax.Array
    kd: object
    k_end: jax.Array
    decay_end: jax.Array
    visible: jax.Array


def _gla_chunks(jobs, kf_ref, bf_ref, *, fast):
    ch = GLA_CHUNK
    ri = lax.broadcasted_iota(jnp.int32, (ch, ch), 0)
    ci = lax.broadcasted_iota(jnp.int32, (ch, ch), 1)
    diag = lax.broadcasted_iota(jnp.int32, (GLA_DK, GLA_DK), 0) == lax.broadcasted_iota(jnp.int32, (GLA_DK, GLA_DK), 1)
    ksls = [slice(hd * GLA_DK, (hd + 1) * GLA_DK) for hd in range(GLA_HEADS)]
    vsls = [slice(hd * GLA_DV, (hd + 1) * GLA_DV) for hd in range(GLA_HEADS)]
    heads = range(GLA_HEADS)

    def operands(job):
        rows = slice(job.row0, job.row0 + ch)
        b = job.b_ref[rows, :]
        q = job.q_ref[rows, :]
        k = job.k_ref[rows, :]
        end = 0 if job.backward else ch - 1
        b_end = b[end:end + 1, :]
        return _GlaOperands(
            rows=rows, b=b, q=q, k=k, v=job.v_ref[rows, :],
            qd=q * jnp.exp2(b).astype(BF16),
            kd=k * jnp.exp2(-b).astype(BF16) if fast else None,
            k_end=k * jnp.exp2(b_end - b).astype(BF16),
            decay_end=jnp.broadcast_to(jnp.exp2(b_end), (GLA_DK, GLA_K_W)),
            visible=(ci >= ri) if job.backward else (ci <= ri))

    def exact_scores(x, ksl):
        kf_ref[...] = x.k.astype(F32)
        bf_ref[...] = x.b
        q_h, b_h = x.q.astype(F32)[:, ksl], x.b[:, ksl]

        def columns(grp, a_acc):
            base = pl.multiple_of(grp * SUBLANES, SUBLANES)
            k_rows = kf_ref[pl.ds(base, SUBLANES), ksl]
            b_rows = bf_ref[pl.ds(base, SUBLANES), ksl]
            for r in range(SUBLANES):
                decay = jnp.exp2(jnp.minimum(b_h - b_rows[r:r + 1, :], 0.0))
                col = jnp.sum(q_h * k_rows[r:r + 1, :] * decay, axis=-1, keepdims=True)
                a_acc = jnp.where(ci == base + r, col, a_acc)
            return a_acc

        return lax.fori_loop(0, ch // SUBLANES, columns, jnp.zeros((ch, ch), F32))

    ops = [operands(job) for job in jobs]
    if fast:
        scores = [[_dot_nt(x.qd[:, ksls[hd]], x.kd[:, ksls[hd]]) for hd in heads] for x in ops]
    else:
        scores = [[exact_scores(x, ksls[hd]) for hd in heads] for x in ops]
    inter = [[_dot(x.qd[:, ksls[hd]], job.s_ref[hd].astype(BF16)) for hd in heads] for job, x in zip(jobs, ops)]
    lhs = [[jnp.concatenate([jnp.where(x.visible, a, 0.0).astype(BF16), x.k_end[:, ksls[hd]].T], axis=0)
            for hd, a in zip(heads, row)] for x, row in zip(ops, scores)]
    both = [[_dot(lhs_h, x.v[:, vsls[hd]]) for hd, lhs_h in zip(heads, row)] for x, row in zip(ops, lhs)]
    for job, x, inter_j, both_j in zip(jobs, ops, inter, both):
        for hd in heads:
            job.o_ref[x.rows, vsls[hd]] = (inter_j[hd] + both_j[hd][:ch]).astype(job.o_ref.dtype)
            decay_col = jnp.sum(jnp.where(diag, x.decay_end[:, ksls[hd]], 0.0), axis=-1, keepdims=True)
            job.s_ref[hd] = job.s_ref[hd] * decay_col + both_j[hd][ch:]


def _gla_kernel(qf_ref, kf_ref, vf_ref, bf_ref, mf_ref, qb_ref, kb_ref, vb_ref, bb_ref, mb_ref, of_ref, ob_ref,
                sf_ref, sb_ref, ktmp_ref, btmp_ref, *, tb):
    @pl.when(pl.program_id(1) == 0)
    def _():
        sf_ref[...] = jnp.zeros_like(sf_ref)
        sb_ref[...] = jnp.zeros_like(sb_ref)

    n_sub = tb // GLA_CHUNK
    both = jnp.minimum(mf_ref[...], mb_ref[...])
    min_b = jnp.min(jnp.min(both, axis=0, keepdims=True), axis=1, keepdims=True)[0, 0]

    def run(fast):
        for s in range(n_sub):
            jobs = [_GlaJob(qf_ref, kf_ref, vf_ref, bf_ref, of_ref, sf_ref, s * GLA_CHUNK, False),
                    _GlaJob(qb_ref, kb_ref, vb_ref, bb_ref, ob_ref, sb_ref, (n_sub - 1 - s) * GLA_CHUNK, True)]
            _gla_chunks(jobs, ktmp_ref, btmp_ref, fast=fast)

    pl.when(min_b >= GLA_SAFE_LOG_DECAY)(lambda: run(True))
    pl.when(min_b < GLA_SAFE_LOG_DECAY)(lambda: run(False))


def _gla(lay, gq, gk, gv, bdec, bmin):
    tb = lay.tb
    assert lay.tm % tb == 0
    tl, ts = lay.l // tb, lay.s // tb
    lat0 = lay.n_ctx // tb

    def fwd_blk(b, t):
        return jnp.where(t < tl, b * tl + t, lat0 + b * ts + (t - tl))

    def bwd_blk(b, t):
        return jnp.where(t < tl, b * tl + (tl - 1 - t), lat0 + b * ts + (ts - 1 - (t - tl)))

    def specs(blk, half):
        return [
            pl.BlockSpec((tb, GLA_K_W), lambda b, t: (blk(b, t), 0)),
            pl.BlockSpec((tb, GLA_K_W), lambda b, t: (blk(b, t), 0)),
            pl.BlockSpec((tb, GLA_V_W), lambda b, t: (blk(b, t), 0)),
            pl.BlockSpec((tb, GLA_K_W), lambda b, t: (blk(b, t), half)),
            pl.BlockSpec((SUBLANES, LANES), lambda b, t: (blk(b, t) // (lay.tm // tb), 0)),
        ]

    return pl.pallas_call(
        functools.partial(_gla_kernel, tb=tb),
        grid=(lay.b, tl + ts),
        in_specs=specs(fwd_blk, 0) + specs(bwd_blk, 1),
        out_specs=[
            pl.BlockSpec((tb, GLA_V_W), lambda b, t: (fwd_blk(b, t), 0)),
            pl.BlockSpec((tb, GLA_V_W), lambda b, t: (bwd_blk(b, t), 0)),
        ],
        out_shape=[jax.ShapeDtypeStruct((lay.n, GLA_V_W), BF16)] * 2,
        scratch_shapes=[
            pltpu.VMEM((GLA_HEADS, GLA_DK, GLA_DV), F32),
            pltpu.VMEM((GLA_HEADS, GLA_DK, GLA_DV), F32),
            pltpu.VMEM((GLA_CHUNK, GLA_K_W), F32),
            pltpu.VMEM((GLA_CHUNK, GLA_K_W), F32),
        ],
        compiler_params=_params("arbitrary", "arbitrary"),
    )(gq, gk, gv, bdec, bmin, gq, gk, gv, bdec, bmin)


def _merge_kernel(x_ref, mod_ref, ya_ref, of_ref, ob_ref, sr_ref, sga_ref, sgg_ref, gnw_ref, wba_ref, wbg_ref, wo_ref,
                  o_ref):
    o = of_ref[...].astype(F32) + ob_ref[...].astype(F32)
    gnw = gnw_ref[...]
    parts = []
    for hd in range(GLA_HEADS):
        oh = o[:, hd * GLA_DV:(hd + 1) * GLA_DV]
        parts.append(oh * lax.rsqrt(jnp.mean(oh * oh, axis=-1, keepdims=True) + NORM_EPS) * gnw)
    y_gla = (jnp.concatenate(parts, axis=-1) * sr_ref[...].astype(F32)).astype(BF16)
    z = (sga_ref[...].astype(F32) * _dot(ya_ref[...], wba_ref[...])
         + sgg_ref[...].astype(F32) * _dot(y_gla, wbg_ref[...]))
    y = _dot(z.astype(BF16), wo_ref[...])
    o_ref[...] = x_ref[...] + _mod_slice(mod_ref, 5) * y


def _merge(lay, layer, x_all, mod, y_attn, o_f, o_b, sr, sga, sgg, gnw, wba, wbg, wo):
    tm = lay.tm
    row = pl.BlockSpec((tm, D_MODEL), lambda i: (i, 0))
    weight = _resident((None, D_MODEL, D_MODEL), lambda i: (layer, 0, 0))
    return pl.pallas_call(
        _merge_kernel,
        grid=(lay.tiles,),
        in_specs=[
            row,
            pl.BlockSpec((None, None, 1, N_MOD * D_MODEL), lambda i: (layer, lay.mod_row(i), 0, 0)),
            row, row, row, row, row, row,
            pl.BlockSpec((None, 1, GLA_DV), lambda i: (layer, 0, 0)),
            weight, weight, weight,
        ],
        out_specs=row,
        out_shape=jax.ShapeDtypeStruct((lay.n, D_MODEL), F32),
        compiler_params=_params("arbitrary"),
    )(x_all, mod, y_attn, o_f, o_b, sr, sga, sgg, gnw, wba, wbg, wo)


def kernel(x, c, ctx, c_ctx, w_mod, b_mod, norm_w, ffn1_w13, ffn1_w2, ffn2_w13, ffn2_w2, w_in, q_norm_w, k_norm_w,
           attn_sink, gla_gate_w_fwd, gla_gate_b_fwd, gla_gate_w_bwd, gla_gate_b_bwd, gla_norm_w, w_branch_attn,
           w_branch_gla, w_out):
    batch, seq, _ = x.shape
    ctx_len = ctx.shape[1]
    depth = w_mod.shape[0]
    assert batch + 1 <= MOD_ROWS
    lay = _Layout(batch, seq, ctx_len)
    lay_ffn = _Layout(batch, seq, ctx_len, FFN_TM_MAX)

    cond_rows = jnp.concatenate([c, c_ctx[None, :], jnp.zeros((MOD_ROWS - batch - 1, D_MODEL), F32)], axis=0)
    ctx_cols = 2 * ATTN_KV_W + GLA_K_W + GLA_V_W
    lr0, lr1 = ctx_cols, ctx_cols + 2 * GLA_RANK
    wm = jnp.concatenate([w_in[:, :, :lr0], w_in[:, :, lr1:]], axis=-1).astype(BF16)
    wlr = jnp.pad(w_in[:, :, lr0:lr1], ((0, 0), (0, 0), (0, LR_PAD - 2 * GLA_RANK))).astype(BF16)
    gw = jnp.zeros((depth, LR_PAD, 2 * GLA_K_W), F32)
    gw = gw.at[:, :GLA_RANK, :GLA_K_W].set(gla_gate_w_fwd).at[:, GLA_RANK:2 * GLA_RANK, GLA_K_W:].set(gla_gate_w_bwd)
    gw = gw.astype(BF16)
    gb = jnp.concatenate([gla_gate_b_fwd, gla_gate_b_bwd], axis=-1)[:, None, :]
    w13_1, w2_1 = ffn1_w13.astype(BF16), ffn1_w2.astype(BF16)
    w13_2, w2_2 = ffn2_w13.astype(BF16), ffn2_w2.astype(BF16)
    wba, wbg, wo = w_branch_attn.astype(BF16), w_branch_gla.astype(BF16), w_out.astype(BF16)
    qnw, knw, gnw = q_norm_w[:, None, :], k_norm_w[:, None, :], gla_norm_w[:, None, :]
    rope_c, rope_s = _rope_tables(lay)
    logit_bound = (HEAD_DIM * QK_SCALE_LOG2 * ATTN_BOUND_SLACK) * (
        jnp.max(jnp.abs(q_norm_w), axis=-1) * jnp.max(jnp.abs(k_norm_w), axis=-1))

    mod = _modulation(cond_rows, w_mod, b_mod).reshape(depth, MOD_ROWS, 1, N_MOD * D_MODEL)
    x_all = (ctx.reshape(batch * ctx_len, D_MODEL), x.reshape(batch * seq, D_MODEL))

    for layer in range(depth):
        last = layer == depth - 1
        x_all = _ffn(lay_ffn, layer, 0, x_all, mod, norm_w, w13_1, w2_1)
        q, kv, gq, gk, gv, bdec, sr, sga, sgg, bmin = _proj(lay, layer, x_all, mod, norm_w, wm, wlr, gw, gb, qnw, knw,
                                                       rope_c, rope_s)
        y_attn = _attn(lay, layer, attn_sink, logit_bound, q, kv)
        o_f, o_b = _gla(lay, gq, gk, gv, bdec, bmin)
        x_all = _merge(lay_ffn, layer, x_all, mod, y_attn, o_f, o_b, sr, sga, sgg, gnw, wba, wbg, wo)
        x_all = _ffn(lay_ffn, layer, 2, x_all, mod, norm_w, w13_2, w2_2, latents_only=last)
    return x_all.reshape(batch, seq, D_MODEL)
```
